```python
import math
import jax, jax.numpy as jnp
from jax import lax
import numpy as np

D_MODEL = 2048
BATCH = 4
SEQ = 4096
DEPTH = 2

HEAD_DIM = 128
FOX_HEADS = 12
FOX_W = FOX_HEADS * HEAD_DIM
MEM_HEADS = 4
MEM_W = MEM_HEADS * HEAD_DIM
MEM_LEN = 256
MIX_W = FOX_W + MEM_W
Q_BLOCK = 128
FOX_IN = 4 * FOX_W + FOX_HEADS + MEM_W

GLA_HEADS = 4
GLA_DV = FOX_W // GLA_HEADS
GLA_DK = GLA_DV // 2
GLA_KW = GLA_HEADS * GLA_DK
GLA_VW = GLA_HEADS * GLA_DV
GLA_RANK = 16
GLA_TAU = 16.0
GLA_CHUNK = 64
GLA_IN = 2 * GLA_KW + 2 * GLA_VW + GLA_RANK + MEM_W

N_GROUPS = 4
EXPERTS_PER_GROUP = 8
N_EXPERTS = N_GROUPS * EXPERTS_PER_GROUP
EXPERT_TOPK = 2
D_EXPERT = 1024
MOE_BLOCK = 128

N_FOX = (DEPTH + 1) // 2
N_GLA = DEPTH // 2
RMS_EPS = 1e-6

kernel_name = "fox_gla_hier_moe_memory_block"


def rmsnorm(x, g):
    xf = x.astype(jnp.float32)
    y = xf * lax.rsqrt(jnp.mean(xf * xf, axis=-1, keepdims=True) + RMS_EPS)
    return (y * g.astype(jnp.float32)).astype(x.dtype)


def split_cols(t, sizes):
    idx = [int(v) for v in np.cumsum(sizes)[:-1]]
    return jnp.split(t, idx, axis=-1)


def split_heads(t, n):
    b, s, w = t.shape
    return t.reshape(b, s, n, w // n).transpose(0, 2, 1, 3)


def merge_heads(t):
    b, h, s, d = t.shape
    return t.transpose(0, 2, 1, 3).reshape(b, s, h * d)


def memory_attention(q_mem, mem_k, mem_v, q_g, k_g):
    q = rmsnorm(split_heads(q_mem, MEM_HEADS), q_g)
    k = rmsnorm(split_heads(mem_k, MEM_HEADS), k_g)
    v = split_heads(mem_v, MEM_HEADS)
    s = jnp.einsum('bhsd,bhmd->bhsm', q, k).astype(jnp.float32) * (1.0 / math.sqrt(HEAD_DIM))
    p = jax.nn.softmax(s, axis=-1).astype(v.dtype)
    return merge_heads(jnp.einsum('bhsm,bhmd->bhsd', p, v))


def forgetting_attention(q, k, v, log_f):
    b, h, s, d = q.shape
    nb = s // Q_BLOCK
    c = jnp.cumsum(log_f, axis=-1)
    qb = q.reshape(b, h, nb, Q_BLOCK, d).transpose(2, 0, 1, 3, 4)
    cb = c.reshape(b, h, nb, Q_BLOCK).transpose(2, 0, 1, 3)
    key_pos = jnp.arange(s)
    scale = 1.0 / math.sqrt(d)

    def block(args):
        qi, ci, i = args
        sc = jnp.einsum('bhqd,bhkd->bhqk', qi, k).astype(jnp.float32) * scale
        sc = sc + ci[..., :, None] - c[:, :, None, :]
        qpos = i * Q_BLOCK + jnp.arange(Q_BLOCK)
        mask = key_pos[None, :] <= qpos[:, None]
        sc = jnp.where(mask, sc, -jnp.inf)
        p = jax.nn.softmax(sc, axis=-1).astype(v.dtype)
        return jnp.einsum('bhqk,bhkd->bhqd', p, v)

    o = lax.map(block, (qb, cb, jnp.arange(nb)))
    return o.transpose(1, 2, 0, 3, 4).reshape(b, h, s, d)


def gla_chunked(q, k, v, log_a):
    b, h, s, dk = q.shape
    dv = v.shape[-1]
    n = s // GLA_CHUNK
    f32 = jnp.float32

    def to_chunks(t):
        return t.reshape(b, h, n, GLA_CHUNK, t.shape[-1]).transpose(2, 0, 1, 3, 4).astype(f32)

    causal = jnp.tril(jnp.ones((GLA_CHUNK, GLA_CHUNK), dtype=bool))

    def step(state, xs):
        qc, kc, vc, gc = xs
        bc = jnp.cumsum(gc, axis=-2)
        b_last = bc[:, :, -1:, :]
        o_inter = jnp.einsum('bhcd,bhde->bhce', qc * jnp.exp(bc), state)
        diff = jnp.where(causal[None, None, :, :, None],
                         bc[:, :, :, None, :] - bc[:, :, None, :, :], -jnp.inf)
        att = jnp.einsum('bhid,bhjd,bhijd->bhij', qc, kc, jnp.exp(diff))
        o_intra = jnp.einsum('bhij,bhje->bhie', att, vc)
        k_dec = kc * jnp.exp(b_last - bc)
        state = jnp.exp(b_last)[:, :, 0, :, None] * state + jnp.einsum('bhcd,bhce->bhde', k_dec, vc)
        return state, o_inter + o_intra

    state0 = jnp.zeros((b, h, dk, dv), f32)
    _, o = lax.scan(step, state0, (to_chunks(q), to_chunks(k), to_chunks(v), to_chunks(log_a)))
    return o.transpose(1, 2, 0, 3, 4).reshape(b, h, s, dv).astype(v.dtype)


def fox_mixer(hn, w_in, b_f, q_g, k_g):
    proj = hn @ w_in
    q, k, v, og, f_logit, q_mem = split_cols(proj, [FOX_W, FOX_W, FOX_W, FOX_W, FOX_HEADS, MEM_W])
    q = rmsnorm(split_heads(q, FOX_HEADS), q_g)
    k = rmsnorm(split_heads(k, FOX_HEADS), k_g)
    v = split_heads(v, FOX_HEADS)
    log_f = jax.nn.log_sigmoid(f_logit.astype(jnp.float32) + b_f.astype(jnp.float32)).transpose(0, 2, 1)
    o = forgetting_attention(q, k, v, log_f)
    return merge_heads(o) * jax.nn.sigmoid(og), q_mem


def gla_mixer(hn, w_in, w_a2, b_a, o_g):
    proj = hn @ w_in
    q, k, v, r, a_lr, q_mem = split_cols(proj, [GLA_KW, GLA_KW, GLA_VW, GLA_VW, GLA_RANK, MEM_W])
    log_a = jax.nn.log_sigmoid((a_lr @ w_a2 + b_a).astype(jnp.float32)) / GLA_TAU
    q = split_heads(q, GLA_HEADS) * (1.0 / math.sqrt(GLA_DK))
    k = split_heads(k, GLA_HEADS)
    v = split_heads(v, GLA_HEADS)
    o = gla_chunked(q, k, v, split_heads(log_a, GLA_HEADS))
    o = rmsnorm(o, o_g)
    return merge_heads(o) * jax.nn.silu(r), q_mem


def hierarchical_moe(hn, w_grp, b_grp, w_exp, b_exp, w_gate, w_up, w_down):
    b, s, d = hn.shape
    t = b * s
    xt = hn.reshape(t, d)
    grp_prob = jax.nn.softmax((xt @ w_grp).astype(jnp.float32) + b_grp.astype(jnp.float32), axis=-1)
    g_p, g_idx = lax.top_k(grp_prob, 1)
    exp_logits = ((xt @ w_exp).astype(jnp.float32) + b_exp.astype(jnp.float32)).reshape(t, N_GROUPS, EXPERTS_PER_GROUP)
    in_grp = exp_logits[jnp.arange(t), g_idx[:, 0]]
    e_val, e_loc = lax.top_k(in_grp, EXPERT_TOPK)
    e_w = jax.nn.softmax(e_val, axis=-1) * g_p
    e_id = g_idx * EXPERTS_PER_GROUP + e_loc

    n = t * EXPERT_TOPK
    flat_e = e_id.reshape(n)
    flat_w = e_w.reshape(n)
    order = jnp.argsort(flat_e)
    sorted_e = flat_e[order]
    tok = (order // EXPERT_TOPK).astype(jnp.int32)
    counts = jnp.zeros((N_EXPERTS,), jnp.int32).at[flat_e].add(1)
    padded = (counts + MOE_BLOCK - 1) // MOE_BLOCK * MOE_BLOCK
    starts = jnp.cumsum(counts) - counts
    pad_ends = jnp.cumsum(padded)
    pad_starts = pad_ends - padded
    dest = pad_starts[sorted_e] + (jnp.arange(n) - starts[sorted_e])
    p_rows = ((n + MOE_BLOCK - 1) // MOE_BLOCK) * MOE_BLOCK + N_EXPERTS * MOE_BLOCK
    nb = p_rows // MOE_BLOCK
    row_tok = jnp.full((p_rows,), t, jnp.int32).at[dest].set(tok)
    row_w = jnp.zeros((p_rows,), jnp.float32).at[dest].set(flat_w[order])
    block_e = jnp.minimum(jnp.searchsorted(pad_ends, jnp.arange(nb) * MOE_BLOCK, side='right'), N_EXPERTS - 1)
    x_pad = jnp.concatenate([xt, jnp.zeros((1, d), xt.dtype)], axis=0)
    xb = x_pad[row_tok].reshape(nb, MOE_BLOCK, d)

    def expert_block(args):
        xi, e = args
        hid = jax.nn.silu(xi @ w_gate[e]) * (xi @ w_up[e])
        return hid @ w_down[e]

    yb = lax.map(expert_block, (xb, block_e)).reshape(p_rows, d)
    yb = yb * row_w[:, None].astype(yb.dtype)
    out = jnp.zeros((t + 1, d), yb.dtype).at[row_tok].add(yb)[:t]
    return out.reshape(b, s, d)


def setup_inputs(seed: int = 0) -> dict:
    key = jax.random.key(seed)
    ks = iter(jax.random.split(key, 32))
    f32 = jnp.float32

    def nrm(shape, scale):
        return jax.random.normal(next(ks), shape, f32) * scale

    def gain(shape):
        return 1.0 + 0.1 * jax.random.normal(next(ks), shape, f32)

    d = D_MODEL
    return {
        "x": nrm((BATCH, SEQ, d), 1.0),
        "mem": nrm((BATCH, MEM_LEN, d), 1.0),
        "mem_norm_g": gain((d,)),
        "w_mem_kv": nrm((d, 2 * MEM_W), d ** -0.5),
        "attn_norm_g": gain((DEPTH, d)),
        "fox_w_in": nrm((N_FOX, d, FOX_IN), d ** -0.5),
        "fox_b_f": 2.0 + 0.5 * jax.random.normal(next(ks), (N_FOX, FOX_HEADS), f32),
        "fox_q_g": gain((N_FOX, HEAD_DIM)),
        "fox_k_g": gain((N_FOX, HEAD_DIM)),
        "gla_w_in": nrm((N_GLA, d, GLA_IN), d ** -0.5),
        "gla_w_a2": nrm((N_GLA, GLA_RANK, GLA_KW), GLA_RANK ** -0.5),
        "gla_b_a": nrm((N_GLA, GLA_KW), 0.5),
        "gla_o_g": gain((N_GLA, GLA_DV)),
        "memq_g": gain((DEPTH, HEAD_DIM)),
        "memk_g": gain((DEPTH, HEAD_DIM)),
        "w_out": nrm((DEPTH, MIX_W, d), MIX_W ** -0.5),
        "ffn_norm_g": gain((DEPTH, d)),
        "w_grp": nrm((DEPTH, d, N_GROUPS), d ** -0.5),
        "b_grp": nrm((DEPTH, N_GROUPS), 0.01),
        "w_exp": nrm((DEPTH, d, N_EXPERTS), d ** -0.5),
        "b_exp": nrm((DEPTH, N_EXPERTS), 0.01),
        "w_gate": nrm((DEPTH, N_EXPERTS, d, D_EXPERT), d ** -0.5),
        "w_up": nrm((DEPTH, N_EXPERTS, d, D_EXPERT), d ** -0.5),
        "w_down": nrm((DEPTH, N_EXPERTS, D_EXPERT, d), D_EXPERT ** -0.5),
    }


def reference(x, mem, mem_norm_g, w_mem_kv, attn_norm_g, fox_w_in, fox_b_f, fox_q_g, fox_k_g,
              gla_w_in, gla_w_a2, gla_b_a, gla_o_g, memq_g, memk_g, w_out, ffn_norm_g,
              w_grp, b_grp, w_exp, b_exp, w_gate, w_up, w_down):
    mem_kv = rmsnorm(mem, mem_norm_g) @ w_mem_kv
    mem_k, mem_v = split_cols(mem_kv, [MEM_W, MEM_W])
    for i in range(DEPTH):
        hn = rmsnorm(x, attn_norm_g[i])
        j = i // 2
        if i % 2 == 0:
            mix, q_mem = fox_mixer(hn, fox_w_in[j], fox_b_f[j], fox_q_g[j], fox_k_g[j])
        else:
            mix, q_mem = gla_mixer(hn, gla_w_in[j], gla_w_a2[j], gla_b_a[j], gla_o_g[j])
        mo = memory_attention(q_mem, mem_k, mem_v, memq_g[i], memk_g[i])
        x = x + jnp.concatenate([mix, mo], axis=-1) @ w_out[i]
        x = x + hierarchical_moe(rmsnorm(x, ffn_norm_g[i]), w_grp[i], b_grp[i], w_exp[i], b_exp[i],
                                 w_gate[i], w_up[i], w_down[i])
    return x
```

```python
import functools
import math

import jax
import jax.numpy as jnp
from jax import lax
from jax.experimental import pallas as pl
from jax.experimental.pallas import tpu as pltpu

F32 = jnp.float32
BF16 = jnp.bfloat16
I32 = jnp.int32

HEAD_DIM = 128
FOX_HEADS = 12
FOX_W = FOX_HEADS * HEAD_DIM
MEM_HEADS = 4
MEM_W = MEM_HEADS * HEAD_DIM
GLA_HEADS = 4
GLA_DV = FOX_W // GLA_HEADS
GLA_DK = GLA_DV // 2
GLA_DKP = 256
GLA_RANK = 16
GLA_TAU = 16.0
N_GROUPS = 4
EXPERTS_PER_GROUP = 8
N_EXPERTS = N_GROUPS * EXPERTS_PER_GROUP
RMS_EPS = 1e-6

LANES = 128
SUBLANES = 8
VMEM_LIMIT_BYTES = 56 * 1024 * 1024

PROJ_TM = 1024
PROJ_TN = 512
ATTN_T = 512
DECAY_TS = 512
MEM_TQ = 512
OUT_TM = 256
ROUTER_TT = 512
ROW_TM = 256
MOE_BLOCK = 256
GLA_TS = 256
GLA_CHUNK = 64
ROUTER_GROUP_ROW = 0
ROUTER_EXPERT_ROW = 8
NEG_BIG = -1e30


def _params(*sem):
    return pltpu.CompilerParams(dimension_semantics=sem, vmem_limit_bytes=VMEM_LIMIT_BYTES)


def _log_sigmoid(z):
    return jnp.minimum(z, 0.0) - jnp.log1p(jnp.exp(-jnp.abs(z)))


def _dot(a, b):
    return jnp.dot(a, b, preferred_element_type=F32)


def _dot_nt(a, b):
    return lax.dot_general(a, b, (((1,), (1,)), ((), ())), preferred_element_type=F32)


def _dot_f32(a, b):
    return jnp.dot(a, b, preferred_element_type=F32, precision=lax.Precision.HIGHEST)


def _split3(x):
    hi = x.astype(BF16)
    r1 = x - hi.astype(F32)
    mid = r1.astype(BF16)
    lo = (r1 - mid.astype(F32)).astype(BF16)
    return hi, mid, lo


def _norm_matmul_kernel(x_ref, g_ref, w_ref, hg_ref, ws_ref, o_ref, os_ref, xn_ref, *, nj_main, n_norm_tiles):
    j = pl.program_id(1)

    @pl.when(j == 0)
    def _():
        x = x_ref[...]
        ms = jnp.mean(x * x, axis=-1, keepdims=True)
        xn_ref[...] = (x * lax.rsqrt(ms + RMS_EPS) * g_ref[...]).astype(BF16)

    if n_norm_tiles > 0:
        @pl.when(j < n_norm_tiles)
        def _():
            acc = _dot(xn_ref[...], w_ref[...])
            for h in range(acc.shape[1] // HEAD_DIM):
                sl = slice(h * HEAD_DIM, (h + 1) * HEAD_DIM)
                a = acc[:, sl]
                ms = jnp.mean(a * a, axis=-1, keepdims=True)
                o_ref[:, sl] = (a * lax.rsqrt(ms + RMS_EPS) * hg_ref[:, sl]).astype(o_ref.dtype)

    @pl.when((j >= n_norm_tiles) & (j < nj_main))
    def _():
        o_ref[...] = _dot(xn_ref[...], w_ref[...]).astype(o_ref.dtype)

    @pl.when(j == nj_main)
    def _():
        os_ref[...] = _dot(xn_ref[...], ws_ref[...])


def _norm_matmul(x, g, w_main, head_gain, w_side, *, n_norm_tiles, tm, name):
    t, d = x.shape
    n_main = w_main.shape[1]
    tn = PROJ_TN
    nj_main = n_main // tn
    assert t % tm == 0 and n_main % tn == 0 and w_side.shape[1] == LANES
    last = nj_main - 1
    kernel = functools.partial(_norm_matmul_kernel, nj_main=nj_main, n_norm_tiles=n_norm_tiles)
    return pl.pallas_call(
        kernel,
        grid=(t // tm, nj_main + 1),
        in_specs=[
            pl.BlockSpec((tm, d), lambda i, j: (i, 0)),
            pl.BlockSpec((1, d), lambda i, j: (0, 0)),
            pl.BlockSpec((d, tn), lambda i, j: (0, jnp.minimum(j, last))),
            pl.BlockSpec((1, tn), lambda i, j: (0, jnp.minimum(j, last))),
            pl.BlockSpec((d, LANES), lambda i, j: (0, 0)),
        ],
        out_specs=[
            pl.BlockSpec((tm, tn), lambda i, j: (i, jnp.minimum(j, last))),
            pl.BlockSpec((tm, LANES), lambda i, j: (i, 0)),
        ],
        out_shape=[jax.ShapeDtypeStruct((t, n_main), BF16), jax.ShapeDtypeStruct((t, LANES), F32)],
        scratch_shapes=[pltpu.VMEM((tm, d), BF16)],
        compiler_params=_params("parallel", "arbitrary"),
        name=name,
    )(x, g.reshape(1, d), w_main, head_gain, w_side)


def _fox_decay_kernel(f_ref, b_ref, c_ref, carry_ref, *, n_rows):
    @pl.when(pl.program_id(1) == 0)
    def _():
        carry_ref[...] = jnp.zeros_like(carry_ref)

    lf = _log_sigmoid(f_ref[...] + b_ref[...])
    lft = lf.T[:n_rows]
    ts = lft.shape[1]
    r = lax.broadcasted_iota(I32, (ts, ts), 0)
    c = lax.broadcasted_iota(I32, (ts, ts), 1)
    incl = jnp.where(r <= c, 1.0, 0.0).astype(BF16)
    hi, mid, lo = _split3(lft)
    cs = _dot(hi, incl) + _dot(mid, incl) + _dot(lo, incl) + carry_ref[:, 0:1]
    c_ref[0] = cs
    carry_ref[...] = jnp.broadcast_to(cs[:, ts - 1:ts], carry_ref.shape)


def _fox_decay(f_side, b_f, batch, seq):
    n_rows = 2 * SUBLANES
    ts = DECAY_TS
    ns = seq // ts
    b_pad = jnp.zeros((1, LANES), F32).at[0, :FOX_HEADS].set(b_f)
    return pl.pallas_call(
        functools.partial(_fox_decay_kernel, n_rows=n_rows),
        grid=(batch, ns),
        in_specs=[
            pl.BlockSpec((ts, LANES), lambda b, s: (b * ns + s, 0)),
            pl.BlockSpec((1, LANES), lambda b, s: (0, 0)),
        ],
        out_specs=pl.BlockSpec((1, n_rows, ts), lambda b, s: (b, 0, s)),
        out_shape=jax.ShapeDtypeStruct((batch, n_rows, seq), F32),
        scratch_shapes=[pltpu.VMEM((n_rows, LANES), F32)],
        compiler_params=_params("parallel", "arbitrary"),
        name="fox_decay",
    )(f_side, b_pad)


def _fox_attn_kernel(qi_tab, ki_tab, q_ref, k_ref, v_ref, og_ref, c_ref, o_ref, m_ref, l_ref, acc_ref):
    step = pl.program_id(2)
    qi = qi_tab[step]
    ki = ki_tab[step]
    tq, tk = q_ref.shape[0], k_ref.shape[0]

    @pl.when(ki == 0)
    def _():
        m_ref[...] = jnp.full_like(m_ref, NEG_BIG)
        l_ref[...] = jnp.zeros_like(l_ref)
        acc_ref[...] = jnp.zeros_like(acc_ref)

    s = _dot_nt(q_ref[...], k_ref[...]) - c_ref[0, 0]
    row = qi * tq + lax.broadcasted_iota(I32, (tq, tk), 0)
    col = ki * tk + lax.broadcasted_iota(I32, (tq, tk), 1)
    s = jnp.where(col <= row, s, NEG_BIG)
    m_prev = m_ref[...]
    m_new = jnp.maximum(m_prev, jnp.max(s, axis=-1, keepdims=True))
    alpha = jnp.exp(m_prev - m_new)
    p = jnp.exp(s - m_new)
    l_ref[...] = alpha * l_ref[...] + jnp.sum(p, axis=-1, keepdims=True)
    acc_ref[...] = alpha * acc_ref[...] + _dot(p.astype(BF16), v_ref[...])
    m_ref[...] = m_new

    @pl.when(ki == qi)
    def _():
        og = og_ref[...].astype(F32)
        o_ref[...] = (acc_ref[...] / l_ref[...] * jax.nn.sigmoid(og)).astype(o_ref.dtype)


def _fox_attention(proj, c4, batch, seq):
    t = ATTN_T
    nq = seq // t
    pairs = [(qi, ki) for qi in range(nq) for ki in range(qi + 1)]
    qi_tab = jnp.asarray([p[0] for p in pairs], I32)
    ki_tab = jnp.asarray([p[1] for p in pairs], I32)
    h_blocks = FOX_HEADS
    grid_spec = pltpu.PrefetchScalarGridSpec(
        num_scalar_prefetch=2,
        grid=(batch, FOX_HEADS, len(pairs)),
        in_specs=[
            pl.BlockSpec((t, HEAD_DIM), lambda b, h, s, qt, kt: (b * nq + qt[s], h)),
            pl.BlockSpec((t, HEAD_DIM), lambda b, h, s, qt, kt: (b * nq + kt[s], h_blocks + h)),
            pl.BlockSpec((t, HEAD_DIM), lambda b, h, s, qt, kt: (b * nq + kt[s], 2 * h_blocks + h)),
            pl.BlockSpec((t, HEAD_DIM), lambda b, h, s, qt, kt: (b * nq + qt[s], 3 * h_blocks + h)),
            pl.BlockSpec((1, 1, 1, t), lambda b, h, s, qt, kt: (b, h, 0, kt[s])),
        ],
        out_specs=pl.BlockSpec((t, HEAD_DIM), lambda b, h, s, qt, kt: (b * nq + qt[s], h)),
        scratch_shapes=[pltpu.VMEM((t, 1), F32), pltpu.VMEM((t, 1), F32), pltpu.VMEM((t, HEAD_DIM), F32)],
    )
    return pl.pallas_call(
        _fox_attn_kernel,
        grid_spec=grid_spec,
        out_shape=jax.ShapeDtypeStruct((batch * seq, FOX_W), BF16),
        compiler_params=_params("parallel", "parallel", "arbitrary"),
        name="fox_attention",
    )(qi_tab, ki_tab, proj, proj, proj, proj, c4)


def _mem_attn_kernel(q_ref, k_ref, v_ref, qg_ref, kg_ref, o_ref):
    for h in range(MEM_HEADS):
        sl = slice(h * HEAD_DIM, (h + 1) * HEAD_DIM)
        q = q_ref[:, sl].astype(F32)
        q = q * lax.rsqrt(jnp.mean(q * q, axis=-1, keepdims=True) + RMS_EPS) * qg_ref[...]
        k = k_ref[:, sl].astype(F32)
        k = k * lax.rsqrt(jnp.mean(k * k, axis=-1, keepdims=True) + RMS_EPS) * kg_ref[...]
        s = _dot_nt(q.astype(BF16), k.astype(BF16))
        p = jnp.exp(s - jnp.max(s, axis=-1, keepdims=True))
        o = _dot(p.astype(BF16), v_ref[:, sl]) / jnp.sum(p, axis=-1, keepdims=True)
        o_ref[:, sl] = o.astype(o_ref.dtype)


def _memory_attention(proj, q_col_block, mem_kv, q_gain, k_gain, batch, seq, mem_len):
    tq = MEM_TQ
    nq = seq // tq
    scale = 1.0 / math.sqrt(HEAD_DIM)
    return pl.pallas_call(
        _mem_attn_kernel,
        grid=(batch * nq,),
        in_specs=[
            pl.BlockSpec((tq, MEM_W), lambda i: (i, q_col_block)),
            pl.BlockSpec((mem_len, MEM_W), lambda i: (i // nq, 0)),
            pl.BlockSpec((mem_len, MEM_W), lambda i: (i // nq, 1)),
            pl.BlockSpec((1, HEAD_DIM), lambda i: (0, 0)),
            pl.BlockSpec((1, HEAD_DIM), lambda i: (0, 0)),
        ],
        out_specs=pl.BlockSpec((tq, MEM_W), lambda i: (i, 0)),
        out_shape=jax.ShapeDtypeStruct((batch * seq, MEM_W), BF16),
        compiler_params=_params("parallel"),
        name="memory_attention",
    )(proj, mem_kv, mem_kv, (q_gain * scale).reshape(1, HEAD_DIM), k_gain.reshape(1, HEAD_DIM))


def _gla_kernel(q_ref, k_ref, v_ref, r_ref, a_ref, wa_ref, ba_ref, og_ref, o_ref, st_ref):
    @pl.when(pl.program_id(1) == 0)
    def _():
        st_ref[...] = jnp.zeros_like(st_ref)

    c = GLA_CHUNK
    n_chunks = q_ref.shape[0] // c
    ri = lax.broadcasted_iota(I32, (c, c), 0)
    ci = lax.broadcasted_iota(I32, (c, c), 1)
    causal = ci <= ri
    incl = jnp.where(causal, 1.0, 0.0)

    def chunk(ic, carry):
        r0 = pl.multiple_of(ic * c, c)
        rows = pl.ds(r0, c)
        a_c = a_ref[rows, :]
        for h in range(GLA_HEADS):
            ks = slice(h * GLA_DKP, (h + 1) * GLA_DKP)
            vs = slice(h * GLA_DV, (h + 1) * GLA_DV)
            z = _dot_f32(a_c, wa_ref[:, ks]) + ba_ref[:, ks]
            g = _log_sigmoid(z) * (1.0 / GLA_TAU)
            bc = _dot_f32(incl, g)
            b_last = bc[c - 1:c, :]
            b_mid = bc[c // 2 - 1:c // 2, :]
            qh = q_ref[rows, ks].astype(F32)
            kh = k_ref[rows, ks].astype(F32)
            vh = v_ref[rows, vs]
            st = st_ref[h]
            o = _dot_nt((qh * jnp.exp(bc)).astype(BF16), st.astype(BF16))
            att = _dot_nt((qh * jnp.exp(bc - b_mid)).astype(BF16), (kh * jnp.exp(b_mid - bc)).astype(BF16))
            att = jnp.where(causal, att, 0.0)
            o = o + _dot(att.astype(BF16), vh)
            k_dec = (kh * jnp.exp(b_last - bc)).astype(BF16)
            upd = lax.dot_general(vh, k_dec, (((0,), (0,)), ((), ())), preferred_element_type=F32)
            st_ref[h] = st * jnp.exp(b_last) + upd
            on = o * lax.rsqrt(jnp.mean(o * o, axis=-1, keepdims=True) + RMS_EPS) * og_ref[...]
            rh = r_ref[rows, vs].astype(F32)
            o_ref[rows, vs] = (on * rh * jax.nn.sigmoid(rh)).astype(o_ref.dtype)
        return carry

    lax.fori_loop(0, n_chunks, chunk, 0)


def _gla(proj, a_side, wa, ba, o_gain, batch, seq):
    ts = GLA_TS
    ns = seq // ts
    kw = GLA_HEADS * GLA_DKP
    vw = GLA_HEADS * GLA_DV
    assert (2 * vw) % kw == 0
    q_blk = (2 * vw) // kw
    return pl.pallas_call(
        _gla_kernel,
        grid=(batch, ns),
        in_specs=[
            pl.BlockSpec((ts, kw), lambda b, s: (b * ns + s, q_blk)),
            pl.BlockSpec((ts, kw), lambda b, s: (b * ns + s, q_blk + 1)),
            pl.BlockSpec((ts, vw), lambda b, s: (b * ns + s, 0)),
            pl.BlockSpec((ts, vw), lambda b, s: (b * ns + s, 1)),
            pl.BlockSpec((ts, LANES), lambda b, s: (b * ns + s, 0)),
            pl.BlockSpec((LANES, kw), lambda b, s: (0, 0)),
            pl.BlockSpec((1, kw), lambda b, s: (0, 0)),
            pl.BlockSpec((1, GLA_DV), lambda b, s: (0, 0)),
        ],
        out_specs=pl.BlockSpec((ts, vw), lambda b, s: (b * ns + s, 0)),
        out_shape=jax.ShapeDtypeStruct((batch * seq, vw), BF16),
        scratch_shapes=[pltpu.VMEM((GLA_HEADS, GLA_DV, GLA_DKP), F32)],
        compiler_params=_params("parallel", "arbitrary"),
        name="gla",
    )(proj, proj, proj, proj, a_side, wa, ba, o_gain.reshape(1, GLA_DV))


def _out_proj_kernel(mix_ref, mo_ref, w1_ref, w2_ref, x_ref, g_ref, wr_ref, xo_ref, hn_ref, lg_ref):
    y = _dot(mix_ref[...], w1_ref[...]) + _dot(mo_ref[...], w2_ref[...])
    xn = x_ref[...] + y
    xo_ref[...] = xn
    hn = xn * lax.rsqrt(jnp.mean(xn * xn, axis=-1, keepdims=True) + RMS_EPS) * g_ref[...]
    hn_ref[...] = hn
    lg_ref[...] = lax.dot_general(wr_ref[...], hn, (((1,), (1,)), ((), ())), preferred_element_type=F32,
                                  precision=lax.Precision.HIGHEST)


def _out_proj(mix, mo, w1, w2, x, g, wr):
    t, d = x.shape
    tm = OUT_TM
    return pl.pallas_call(
        _out_proj_kernel,
        grid=(t // tm,),
        in_specs=[
            pl.BlockSpec((tm, mix.shape[1]), lambda i: (i, 0)),
            pl.BlockSpec((tm, mo.shape[1]), lambda i: (i, 0)),
            pl.BlockSpec(w1.shape, lambda i: (0, 0)),
            pl.BlockSpec(w2.shape, lambda i: (0, 0)),
            pl.BlockSpec((tm, d), lambda i: (i, 0)),
            pl.BlockSpec((1, d), lambda i: (0, 0)),
            pl.BlockSpec((LANES, d), lambda i: (0, 0)),
        ],
        out_specs=[
            pl.BlockSpec((tm, d), lambda i: (i, 0)),
            pl.BlockSpec((tm, d), lambda i: (i, 0)),
            pl.BlockSpec((LANES, tm), lambda i: (0, i)),
        ],
        out_shape=[jax.ShapeDtypeStruct((t, d), F32), jax.ShapeDtypeStruct((t, d), F32),
                   jax.ShapeDtypeStruct((LANES, t), F32)],
        compiler_params=_params("parallel"),
        name="out_proj",
    )(mix, mo, w1, w2, x, g.reshape(1, d), wr)


def _router_kernel(lg_ref, b_ref, o_ref, cnt_ref, carry_ref):
    @pl.when(pl.program_id(0) == 0)
    def _():
        carry_ref[...] = jnp.zeros_like(carry_ref)

    logits = lg_ref[...] + b_ref[:, 0:1]
    tt = logits.shape[1]
    gl = [logits[ROUTER_GROUP_ROW + g:ROUTER_GROUP_ROW + g + 1, :] for g in range(N_GROUPS)]
    best = gl[0]
    gidx = jnp.zeros((1, tt), I32)
    for g in range(1, N_GROUPS):
        better = gl[g] > best
        best = jnp.where(better, gl[g], best)
        gidx = jnp.where(better, g, gidx)
    denom = jnp.exp(gl[0] - best)
    for g in range(1, N_GROUPS):
        denom = denom + jnp.exp(gl[g] - best)
    g_p = 1.0 / denom

    epg = EXPERTS_PER_GROUP
    in_grp = logits[ROUTER_EXPERT_ROW:ROUTER_EXPERT_ROW + epg, :]
    for g in range(1, N_GROUPS):
        lo = ROUTER_EXPERT_ROW + g * epg
        in_grp = jnp.where(gidx == g, logits[lo:lo + epg, :], in_grp)
    rows = lax.broadcasted_iota(I32, (epg, tt), 0).astype(F32)
    m1 = jnp.max(in_grp, axis=0, keepdims=True)
    i1 = jnp.min(jnp.where(in_grp == m1, rows, float(epg)), axis=0, keepdims=True)
    rest = jnp.where(rows == i1, -jnp.inf, in_grp)
    m2 = jnp.max(rest, axis=0, keepdims=True)
    i2 = jnp.min(jnp.where(rest == m2, rows, float(epg)), axis=0, keepdims=True)
    e21 = jnp.exp(m2 - m1)
    w1 = g_p / (1.0 + e21)
    w2 = g_p * e21 / (1.0 + e21)
    e1 = gidx * epg + i1.astype(I32)
    e2 = gidx * epg + i2.astype(I32)

    er = lax.broadcasted_iota(I32, (N_EXPERTS, tt), 0)
    oh1 = er == e1
    oh2 = er == e2
    cnt = jnp.where(oh1, 1.0, 0.0) + jnp.where(oh2, 1.0, 0.0)
    r = lax.broadcasted_iota(I32, (tt, tt), 0)
    c = lax.broadcasted_iota(I32, (tt, tt), 1)
    strict = jnp.where(r < c, 1.0, 0.0).astype(BF16)
    before = _dot(cnt.astype(BF16), strict) + carry_ref[:, 0:1]
    rank1 = jnp.sum(jnp.where(oh1, before, 0.0), axis=0, keepdims=True)
    rank2 = jnp.sum(jnp.where(oh2, before, 0.0), axis=0, keepdims=True)
    total = carry_ref[...] + jnp.sum(cnt, axis=1, keepdims=True)
    carry_ref[...] = total
    cnt_ref[...] = total

    o_ref[0:1, :] = e1.astype(F32)
    o_ref[1:2, :] = e2.astype(F32)
    o_ref[2:3, :] = rank1
    o_ref[3:4, :] = rank2
    o_ref[4:5, :] = w1
    o_ref[5:6, :] = w2
    o_ref[6:8, :] = jnp.zeros((2, tt), F32)


def _router(logits_t, bias_col):
    t = logits_t.shape[1]
    tt = ROUTER_TT
    return pl.pallas_call(
        _router_kernel,
        grid=(t // tt,),
        in_specs=[
            pl.BlockSpec((LANES, tt), lambda i: (0, i)),
            pl.BlockSpec((LANES, LANES), lambda i: (0, 0)),
        ],
        out_specs=[
            pl.BlockSpec((SUBLANES, tt), lambda i: (0, i)),
            pl.BlockSpec((N_EXPERTS, LANES), lambda i: (0, 0)),
        ],
        out_shape=[jax.ShapeDtypeStruct((SUBLANES, t), F32), jax.ShapeDtypeStruct((N_EXPERTS, LANES), F32)],
        scratch_shapes=[pltpu.VMEM((N_EXPERTS, LANES), F32)],
        compiler_params=_params("arbitrary"),
        name="router",
    )(logits_t, bias_col)


def _row_copy(src_ref, src_row, dst_ref, dst_row, sem):
    return pltpu.make_async_copy(src_ref.at[pl.ds(src_row, 1), :], dst_ref.at[pl.ds(dst_row, 1), :], sem)


def _dispatch_kernel(d1_ref, d2_ref, h_ref, xs_in_ref, xs_ref, sem):
    del xs_in_ref
    tm = h_ref.shape[0]
    base = pl.program_id(0) * tm

    def issue(r, carry):
        _row_copy(h_ref, r, xs_ref, d1_ref[base + r], sem).start()
        _row_copy(h_ref, r, xs_ref, d2_ref[base + r], sem).start()
        return carry

    lax.fori_loop(0, tm, issue, 0)

    def drain(r, carry):
        _row_copy(h_ref, r, xs_ref, 0, sem).wait()
        _row_copy(h_ref, r, xs_ref, 0, sem).wait()
        return carry

    lax.fori_loop(0, tm, drain, 0)


def _dispatch(dest1, dest2, hn, p_rows):
    t, d = hn.shape
    tm = ROW_TM
    grid_spec = pltpu.PrefetchScalarGridSpec(
        num_scalar_prefetch=2,
        grid=(t // tm,),
        in_specs=[
            pl.BlockSpec((tm, d), lambda i, a, b: (i, 0)),
            pl.BlockSpec(memory_space=pl.ANY),
        ],
        out_specs=pl.BlockSpec(memory_space=pl.ANY),
        scratch_shapes=[pltpu.SemaphoreType.DMA],
    )
    return pl.pallas_call(
        _dispatch_kernel,
        grid_spec=grid_spec,
        out_shape=jax.ShapeDtypeStruct((p_rows, d), hn.dtype),
        input_output_aliases={3: 0},
        compiler_params=_params("arbitrary"),
        name="moe_dispatch",
    )(dest1, dest2, hn, jnp.zeros((p_rows, d), hn.dtype))


def _moe_kernel(be_ref, valid_ref, x_ref, wg_ref, wu_ref, wd_ref, y_ref):
    b = pl.program_id(0)

    @pl.when(valid_ref[b] != 0)
    def _():
        x = x_ref[...].astype(BF16)
        g = _dot(x, wg_ref[0])
        u = _dot(x, wu_ref[0])
        hid = (g * jax.nn.sigmoid(g) * u).astype(BF16)
        y_ref[...] = _dot(hid, wd_ref[0])

    @pl.when(valid_ref[b] == 0)
    def _():
        y_ref[...] = jnp.zeros_like(y_ref)


def _moe(block_e, block_valid, xs, wg, wu, wd):
    p_rows, d = xs.shape
    tm = MOE_BLOCK
    de = wg.shape[2]
    grid_spec = pltpu.PrefetchScalarGridSpec(
        num_scalar_prefetch=2,
        grid=(p_rows // tm,),
        in_specs=[
            pl.BlockSpec((tm, d), lambda i, be, bv: (i, 0)),
            pl.BlockSpec((1, d, de), lambda i, be, bv: (be[i], 0, 0)),
            pl.BlockSpec((1, d, de), lambda i, be, bv: (be[i], 0, 0)),
            pl.BlockSpec((1, de, d), lambda i, be, bv: (be[i], 0, 0)),
        ],
        out_specs=pl.BlockSpec((tm, d), lambda i, be, bv: (i, 0)),
    )
    return pl.pallas_call(
        _moe_kernel,
        grid_spec=grid_spec,
        out_shape=jax.ShapeDtypeStruct((p_rows, d), F32),
        compiler_params=_params("arbitrary"),
        name="moe_experts",
    )(block_e, block_valid, xs, wg, wu, wd)


def _combine_kernel(d1_ref, d2_ref, x_ref, w_ref, ys_ref, o_ref, buf1, buf2, sem):
    tm = x_ref.shape[0]
    base = pl.program_id(0) * tm

    def issue(r, carry):
        _row_copy(ys_ref, d1_ref[base + r], buf1, r, sem).start()
        _row_copy(ys_ref, d2_ref[base + r], buf2, r, sem).start()
        return carry

    lax.fori_loop(0, tm, issue, 0)

    def drain(r, carry):
        _row_copy(ys_ref, 0, buf1, r, sem).wait()
        _row_copy(ys_ref, 0, buf2, r, sem).wait()
        return carry

    lax.fori_loop(0, tm, drain, 0)
    w = w_ref[...]
    o_ref[...] = x_ref[...] + w[:, 0:1] * buf1[...] + w[:, 1:2] * buf2[...]


def _combine(dest1, dest2, x, w12, ys):
    t, d = x.shape
    tm = ROW_TM
    grid_spec = pltpu.PrefetchScalarGridSpec(
        num_scalar_prefetch=2,
        grid=(t // tm,),
        in_specs=[
            pl.BlockSpec((tm, d), lambda i, a, b: (i, 0)),
            pl.BlockSpec((tm, 2), lambda i, a, b: (i, 0)),
            pl.BlockSpec(memory_space=pl.ANY),
        ],
        out_specs=pl.BlockSpec((tm, d), lambda i, a, b: (i, 0)),
        scratch_shapes=[pltpu.VMEM((tm, d), F32), pltpu.VMEM((tm, d), F32), pltpu.SemaphoreType.DMA],
    )
    return pl.pallas_call(
        _combine_kernel,
        grid_spec=grid_spec,
        out_shape=jax.ShapeDtypeStruct((t, d), F32),
        compiler_params=_params("arbitrary"),
        name="moe_combine",
    )(dest1, dest2, x, w12, ys)


def _pad_cols(w, width):
    return jnp.pad(w, ((0, 0), (0, width - w.shape[1])))


def _moe_layer(x_mid, hn, logits_t, b_grp, b_exp, w_gate, w_up, w_down):
    t, d = x_mid.shape
    bias = jnp.zeros((LANES,), F32)
    bias = bias.at[ROUTER_GROUP_ROW:ROUTER_GROUP_ROW + N_GROUPS].set(b_grp)
    bias = bias.at[ROUTER_EXPERT_ROW:ROUTER_EXPERT_ROW + N_EXPERTS].set(b_exp)
    routed, totals = _router(logits_t, jnp.broadcast_to(bias[:, None], (LANES, LANES)))
    e1 = routed[0].astype(I32)
    e2 = routed[1].astype(I32)
    counts = totals[:, 0].astype(I32)
    padded = (counts + MOE_BLOCK - 1) // MOE_BLOCK * MOE_BLOCK
    pad_ends = jnp.cumsum(padded)
    pad_starts = pad_ends - padded
    dest1 = pad_starts[e1] + routed[2].astype(I32)
    dest2 = pad_starts[e2] + routed[3].astype(I32)
    p_rows = 2 * t + N_EXPERTS * MOE_BLOCK
    nb = p_rows // MOE_BLOCK
    block_row = jnp.arange(nb, dtype=I32) * MOE_BLOCK
    block_e = jnp.minimum(jnp.searchsorted(pad_ends, block_row, side="right"), N_EXPERTS - 1).astype(I32)
    block_valid = (block_row < pad_ends[-1]).astype(I32)
    xs = _dispatch(dest1, dest2, hn, p_rows)
    ys = _moe(block_e, block_valid, xs, w_gate.astype(BF16), w_up.astype(BF16), w_down.astype(BF16))
    w12 = jnp.stack([routed[4], routed[5]], axis=-1)
    return _combine(dest1, dest2, x_mid, w12, ys)


def _router_weight(w_grp, w_exp):
    d = w_grp.shape[0]
    wr = jnp.zeros((LANES, d), F32)
    wr = wr.at[ROUTER_GROUP_ROW:ROUTER_GROUP_ROW + N_GROUPS].set(w_grp.T)
    return wr.at[ROUTER_EXPERT_ROW:ROUTER_EXPERT_ROW + N_EXPERTS].set(w_exp.T)


def _fox_layer(x2, attn_g, w_in, b_f, q_g, k_g, mem_kv, memq_g, memk_g, batch, seq, mem_len):
    qkvo = 4 * FOX_W
    w_main = jnp.concatenate([w_in[:, :qkvo], w_in[:, qkvo + FOX_HEADS:]], axis=1).astype(BF16)
    w_side = _pad_cols(w_in[:, qkvo:qkvo + FOX_HEADS], LANES).astype(BF16)
    scale = 1.0 / math.sqrt(HEAD_DIM)
    head_gain = jnp.concatenate([jnp.tile(q_g * scale, FOX_HEADS), jnp.tile(k_g, FOX_HEADS),
                                 jnp.ones((w_main.shape[1] - 2 * FOX_W,), F32)]).reshape(1, -1)
    proj, f_side = _norm_matmul(x2, attn_g, w_main, head_gain, w_side, n_norm_tiles=2 * FOX_W // PROJ_TN,
                                tm=PROJ_TM, name="fox_in_proj")
    c = _fox_decay(f_side, b_f, batch, seq)
    mix = _fox_attention(proj, c.reshape(batch, c.shape[1], 1, seq), batch, seq)
    mo = _memory_attention(proj, qkvo // MEM_W, mem_kv, memq_g, memk_g, batch, seq, mem_len)
    return mix, mo


def _gla_layer(x2, attn_g, w_in, w_a2, b_a, o_g, mem_kv, memq_g, memk_g, batch, seq, mem_len):
    d = w_in.shape[0]
    kw = GLA_HEADS * GLA_DK
    vw = GLA_HEADS * GLA_DV
    pad = GLA_DKP - GLA_DK

    def pad_heads(w):
        lead = w.shape[0]
        return jnp.pad(w.reshape(lead, GLA_HEADS, GLA_DK), ((0, 0), (0, 0), (0, pad))).reshape(lead, -1)

    wq = pad_heads(w_in[:, :kw] * (1.0 / math.sqrt(GLA_DK)))
    wk = pad_heads(w_in[:, kw:2 * kw])
    rest = w_in[:, 2 * kw:2 * kw + 2 * vw]
    w_a1 = w_in[:, 2 * kw + 2 * vw:2 * kw + 2 * vw + GLA_RANK]
    w_qm = w_in[:, 2 * kw + 2 * vw + GLA_RANK:]
    w_main = jnp.concatenate([rest, wq, wk, w_qm], axis=1).astype(BF16)
    w_side = _pad_cols(w_a1, LANES).astype(BF16)
    head_gain = jnp.ones((1, w_main.shape[1]), F32)
    proj, a_side = _norm_matmul(x2, attn_g, w_main, head_gain, w_side, n_norm_tiles=0, tm=PROJ_TM,
                                name="gla_in_proj")
    wa = jnp.pad(pad_heads(w_a2), ((0, LANES - GLA_RANK), (0, 0)))
    ba = pad_heads(b_a.reshape(1, kw))
    mix = _gla(proj, a_side, wa, ba, o_g, batch, seq)
    q_col_block = (2 * GLA_HEADS * GLA_DKP + 2 * vw) // MEM_W
    mo = _memory_attention(proj, q_col_block, mem_kv, memq_g, memk_g, batch, seq, mem_len)
    return mix, mo


def kernel(x, mem, mem_norm_g, w_mem_kv, attn_norm_g, fox_w_in, fox_b_f, fox_q_g, fox_k_g, gla_w_in, gla_w_a2,
           gla_b_a, gla_o_g, memq_g, memk_g, w_out, ffn_norm_g, w_grp, b_grp, w_exp, b_exp, w_gate, w_up, w_down):
    batch, seq, d = x.shape
    mem_len = mem.shape[1]
    depth = attn_norm_g.shape[0]
    x2 = x.reshape(batch * seq, d)

    n_kv = w_mem_kv.shape[1]
    mem_kv, _ = _norm_matmul(mem.reshape(batch * mem_len, d), mem_norm_g, w_mem_kv.astype(BF16),
                             jnp.ones((1, n_kv), F32), jnp.zeros((d, LANES), BF16), n_norm_tiles=0,
                             tm=batch * mem_len, name="mem_kv_proj")

    for i in range(depth):
        j = i // 2
        if i % 2 == 0:
            mix, mo = _fox_layer(x2, attn_norm_g[i], fox_w_in[j], fox_b_f[j], fox_q_g[j], fox_k_g[j],
                                 mem_kv, memq_g[i], memk_g[i], batch, seq, mem_len)
        else:
            mix, mo = _gla_layer(x2, attn_norm_g[i], gla_w_in[j], gla_w_a2[j], gla_b_a[j], gla_o_g[j],
                                 mem_kv, memq_g[i], memk_g[i], batch, seq, mem_len)
        w_o = w_out[i].astype(BF16)
        x_mid, hn, logits_t = _out_proj(mix, mo, w_o[:mix.shape[1]], w_o[mix.shape[1]:], x2, ffn_norm_g[i],
                                        _router_weight(w_grp[i], w_exp[i]))
        x2 = _moe_layer(x_mid, hn, logits_t, b_grp[i], b_exp[i], w_gate[i], w_up[i], w_down[i])
    return x2.reshape(batch, seq, d)
```

```python
import functools
import math

import jax
import jax.numpy as jnp
from jax import lax
from jax.experimental import pallas as pl
from jax.experimental.pallas import tpu as pltpu

F32 = jnp.float32
BF16 = jnp.bfloat16
I32 = jnp.int32

HEAD_DIM = 128
FOX_HEADS = 12
FOX_W = FOX_HEADS * HEAD_DIM
MEM_HEADS = 4
MEM_W = MEM_HEADS * HEAD_DIM
GLA_HEADS = 4
GLA_DV = FOX_W // GLA_HEADS
GLA_DK = GLA_DV // 2
GLA_DKP = 256
GLA_RANK = 16
GLA_TAU = 16.0
N_GROUPS = 4
EXPERTS_PER_GROUP = 8
N_EXPERTS = N_GROUPS * EXPERTS_PER_GROUP
RMS_EPS = 1e-6
LOG2E = 1.4426950408889634
GATE_GROUP = 16

LANES = 128
SUBLANES = 8
VMEM_LIMIT_BYTES = 56 * 1024 * 1024

PROJ_TM = 1024
PROJ_TN = 512
ATTN_T = 512
ATTN_HEADS_PER_STEP = 6
DECAY_TS = 512
MEM_TQ = 512
OUT_TM = 512
ROUTER_TT = 512
ROW_TM = 256
MOE_BLOCK = 256
GLA_TS = 256
GLA_CHUNK = 64
ROUTER_GROUP_ROW = 0
ROUTER_EXPERT_ROW = 8
NEG_BIG = -1e30


def _params(*sem):
    return pltpu.CompilerParams(dimension_semantics=sem, vmem_limit_bytes=VMEM_LIMIT_BYTES)


def _log_sigmoid(z):
    return jnp.minimum(z, 0.0) - jnp.log1p(jnp.exp(-jnp.abs(z)))


def _dot(a, b):
    return jnp.dot(a, b, preferred_element_type=F32)


def _dot_nt(a, b):
    return lax.dot_general(a, b, (((1,), (1,)), ((), ())), preferred_element_type=F32)


def _dot_f32(a, b):
    return jnp.dot(a, b, preferred_element_type=F32, precision=lax.Precision.HIGHEST)


def _split3(x):
    hi = x.astype(BF16)
    r1 = x - hi.astype(F32)
    mid = r1.astype(BF16)
    lo = (r1 - mid.astype(F32)).astype(BF16)
    return hi, mid, lo


def _norm_matmul_kernel(x_ref, g_ref, w_ref, hg_ref, ws_ref, o_ref, os_ref, xn_ref, *, nj_main, n_norm_tiles):
    j = pl.program_id(1)

    @pl.when(j == 0)
    def _():
        x = x_ref[...]
        ms = jnp.mean(x * x, axis=-1, keepdims=True)
        xn_ref[...] = (x * lax.rsqrt(ms + RMS_EPS) * g_ref[...]).astype(BF16)

    if n_norm_tiles > 0:
        @pl.when(j < n_norm_tiles)
        def _():
            acc = _dot(xn_ref[...], w_ref[...])
            for h in range(acc.shape[1] // HEAD_DIM):
                sl = slice(h * HEAD_DIM, (h + 1) * HEAD_DIM)
                a = acc[:, sl]
                ms = jnp.mean(a * a, axis=-1, keepdims=True)
                o_ref[:, sl] = (a * lax.rsqrt(ms + RMS_EPS) * hg_ref[:, sl]).astype(o_ref.dtype)

    @pl.when((j >= n_norm_tiles) & (j < nj_main))
    def _():
        o_ref[...] = _dot(xn_ref[...], w_ref[...]).astype(o_ref.dtype)

    @pl.when(j == nj_main)
    def _():
        os_ref[...] = _dot(xn_ref[...], ws_ref[...])


def _norm_matmul(x, g, w_main, head_gain, w_side, *, n_norm_tiles, tm, name):
    t, d = x.shape
    n_main = w_main.shape[1]
    tn = PROJ_TN
    nj_main = n_main // tn
    assert t % tm == 0 and n_main % tn == 0 and w_side.shape[1] == LANES
    last = nj_main - 1
    kernel = functools.partial(_norm_matmul_kernel, nj_main=nj_main, n_norm_tiles=n_norm_tiles)
    return pl.pallas_call(
        kernel,
        grid=(t // tm, nj_main + 1),
        in_specs=[
            pl.BlockSpec((tm, d), lambda i, j: (i, 0)),
            pl.BlockSpec((1, d), lambda i, j: (0, 0)),
            pl.BlockSpec((d, tn), lambda i, j: (0, jnp.minimum(j, last))),
            pl.BlockSpec((1, tn), lambda i, j: (0, jnp.minimum(j, last))),
            pl.BlockSpec((d, LANES), lambda i, j: (0, 0)),
        ],
        out_specs=[
            pl.BlockSpec((tm, tn), lambda i, j: (i, jnp.minimum(j, last))),
            pl.BlockSpec((tm, LANES), lambda i, j: (i, 0)),
        ],
        out_shape=[jax.ShapeDtypeStruct((t, n_main), BF16), jax.ShapeDtypeStruct((t, LANES), F32)],
        scratch_shapes=[pltpu.VMEM((tm, d), BF16)],
        compiler_params=_params("parallel", "arbitrary"),
        name=name,
    )(x, g.reshape(1, d), w_main, head_gain, w_side)


def _fox_decay_kernel(f_ref, b_ref, kb_ref, carry_ref):
    @pl.when(pl.program_id(1) == 0)
    def _():
        carry_ref[...] = jnp.zeros_like(carry_ref)

    lf = _log_sigmoid(f_ref[...] + b_ref[...]) * (-LOG2E)
    ts = lf.shape[0]
    r = lax.broadcasted_iota(I32, (ts, ts), 0)
    c = lax.broadcasted_iota(I32, (ts, ts), 1)
    incl = jnp.where(c <= r, 1.0, 0.0).astype(BF16)
    hi, mid, lo = _split3(lf)
    cs = _dot(incl, hi) + _dot(incl, mid) + _dot(incl, lo) + carry_ref[0:1, :]
    carry_ref[...] = jnp.broadcast_to(cs[ts - 1:ts, :], carry_ref.shape)
    hi, mid, lo = _split3(cs)
    lane = lax.broadcasted_iota(I32, cs.shape, 1)
    piece = jnp.where(lane < GATE_GROUP, hi.astype(F32),
                      jnp.where(lane < 2 * GATE_GROUP, mid.astype(F32),
                                jnp.where(lane < 3 * GATE_GROUP, lo.astype(F32), 0.0)))
    kb_ref[...] = piece.astype(BF16)


def _fox_decay(f_side, b_f, batch, seq):
    ts = DECAY_TS
    ns = seq // ts
    b_pad = jnp.zeros((LANES,), F32)
    for p in range(3):
        b_pad = b_pad.at[p * GATE_GROUP:p * GATE_GROUP + FOX_HEADS].set(b_f)
    return pl.pallas_call(
        _fox_decay_kernel,
        grid=(batch, ns),
        in_specs=[
            pl.BlockSpec((ts, LANES), lambda b, s: (b * ns + s, 0)),
            pl.BlockSpec((1, LANES), lambda b, s: (0, 0)),
        ],
        out_specs=pl.BlockSpec((ts, LANES), lambda b, s: (b * ns + s, 0)),
        out_shape=jax.ShapeDtypeStruct((batch * seq, LANES), BF16),
        scratch_shapes=[pltpu.VMEM((SUBLANES, LANES), F32)],
        compiler_params=_params("parallel", "arbitrary"),
        name="fox_decay",
    )(f_side, b_pad.reshape(1, LANES))


def _fox_attn_kernel(q_ref, k_ref, v_ref, kb_ref, og_ref, o_ref, m_ref, acc_ref):
    hp = pl.program_id(1)
    qi = pl.program_id(2)
    t = q_ref.shape[0]
    lane = lax.broadcasted_iota(I32, (t, LANES), 1)
    ones = jnp.ones((t, HEAD_DIM), BF16)
    q_aug = []
    for hh in range(ATTN_HEADS_PER_STEP):
        h = hp * ATTN_HEADS_PER_STEP + hh
        own = jnp.where(lane < 3 * GATE_GROUP, jnp.where((lane & (GATE_GROUP - 1)) == h, 1.0, 0.0), 0.0)
        q_aug.append(jnp.concatenate([q_ref[:, hh * HEAD_DIM:(hh + 1) * HEAD_DIM], own.astype(BF16)], axis=1))
    m_ref[...] = jnp.full_like(m_ref, NEG_BIG)
    acc_ref[...] = jnp.zeros_like(acc_ref)

    def tile(j, masked):
        rows = pl.ds(pl.multiple_of(j * t, t), t)
        kb = kb_ref[rows, :]
        for hh in range(ATTN_HEADS_PER_STEP):
            cols = slice(hh * HEAD_DIM, (hh + 1) * HEAD_DIM)
            k_aug = jnp.concatenate([k_ref[rows, cols], kb], axis=1)
            s = _dot_nt(q_aug[hh], k_aug)
            if masked:
                row = lax.broadcasted_iota(I32, (t, t), 0)
                col = lax.broadcasted_iota(I32, (t, t), 1)
                s = jnp.where(col <= row, s, NEG_BIG)
            m_prev = m_ref[hh]
            m_new = jnp.maximum(m_prev, jnp.max(s, axis=-1, keepdims=True))
            alpha = jnp.exp2(m_prev - m_new)
            p = jnp.exp2(s - jnp.concatenate([m_new] * (t // LANES), axis=1))
            v_aug = jnp.concatenate([v_ref[rows, cols], ones], axis=1)
            pv = _dot(p.astype(BF16), v_aug)
            acc_ref[hh] = jnp.concatenate([alpha, alpha], axis=1) * acc_ref[hh] + pv
            m_ref[hh] = m_new

    def full_tile(j, carry):
        tile(j, False)
        return carry

    lax.fori_loop(0, qi, full_tile, 0)
    tile(qi, True)
    for hh in range(ATTN_HEADS_PER_STEP):
        cols = slice(hh * HEAD_DIM, (hh + 1) * HEAD_DIM)
        acc = acc_ref[hh]
        og = og_ref[:, cols].astype(F32)
        o_ref[:, cols] = (acc[:, :HEAD_DIM] / acc[:, HEAD_DIM:] * jax.nn.sigmoid(og)).astype(o_ref.dtype)


def _fox_attention(proj, kb, batch, seq):
    t = ATTN_T
    nq = seq // t
    hps = ATTN_HEADS_PER_STEP
    w = hps * HEAD_DIM
    nhp = FOX_HEADS // hps
    return pl.pallas_call(
        _fox_attn_kernel,
        grid=(batch, nhp, nq),
        in_specs=[
            pl.BlockSpec((t, w), lambda b, h, i: (b * nq + i, h)),
            pl.BlockSpec((seq, w), lambda b, h, i: (b, nhp + h)),
            pl.BlockSpec((seq, w), lambda b, h, i: (b, 2 * nhp + h)),
            pl.BlockSpec((seq, LANES), lambda b, h, i: (b, 0)),
            pl.BlockSpec((t, w), lambda b, h, i: (b * nq + i, 3 * nhp + h)),
        ],
        out_specs=pl.BlockSpec((t, w), lambda b, h, i: (b * nq + i, h)),
        out_shape=jax.ShapeDtypeStruct((batch * seq, FOX_W), BF16),
        scratch_shapes=[pltpu.VMEM((hps, t, LANES), F32), pltpu.VMEM((hps, t, 2 * HEAD_DIM), F32)],
        compiler_params=_params("parallel", "parallel", "arbitrary"),
        name="fox_attention",
    )(proj, proj, proj, kb, proj)


def _mem_attn_kernel(q_ref, k_ref, v_ref, qg_ref, kg_ref, o_ref):
    for h in range(MEM_HEADS):
        sl = slice(h * HEAD_DIM, (h + 1) * HEAD_DIM)
        q = q_ref[:, sl].astype(F32)
        q = q * lax.rsqrt(jnp.mean(q * q, axis=-1, keepdims=True) + RMS_EPS) * qg_ref[...]
        k = k_ref[:, sl].astype(F32)
        k = k * lax.rsqrt(jnp.mean(k * k, axis=-1, keepdims=True) + RMS_EPS) * kg_ref[...]
        s = _dot_nt(q.astype(BF16), k.astype(BF16))
        p = jnp.exp(s - jnp.max(s, axis=-1, keepdims=True))
        o = _dot(p.astype(BF16), v_ref[:, sl]) / jnp.sum(p, axis=-1, keepdims=True)
        o_ref[:, sl] = o.astype(o_ref.dtype)


def _memory_attention(proj, q_col_block, mem_kv, q_gain, k_gain, batch, seq, mem_len):
    tq = MEM_TQ
    nq = seq // tq
    scale = 1.0 / math.sqrt(HEAD_DIM)
    return pl.pallas_call(
        _mem_attn_kernel,
        grid=(batch * nq,),
        in_specs=[
            pl.BlockSpec((tq, MEM_W), lambda i: (i, q_col_block)),
            pl.BlockSpec((mem_len, MEM_W), lambda i: (i // nq, 0)),
            pl.BlockSpec((mem_len, MEM_W), lambda i: (i // nq, 1)),
            pl.BlockSpec((1, HEAD_DIM), lambda i: (0, 0)),
            pl.BlockSpec((1, HEAD_DIM), lambda i: (0, 0)),
        ],
        out_specs=pl.BlockSpec((tq, MEM_W), lambda i: (i, 0)),
        out_shape=jax.ShapeDtypeStruct((batch * seq, MEM_W), BF16),
        compiler_params=_params("parallel"),
        name="memory_attention",
    )(proj, mem_kv, mem_kv, (q_gain * scale).reshape(1, HEAD_DIM), k_gain.reshape(1, HEAD_DIM))


def _gla_kernel(q_ref, k_ref, v_ref, r_ref, a_ref, wa_ref, ba_ref, og_ref, o_ref,
                st_ref, qe_ref, q2_ref, k2_ref, kd_ref, oi_ref):
    @pl.when(pl.program_id(1) == 0)
    def _():
        st_ref[...] = jnp.zeros_like(st_ref)

    ts = q_ref.shape[0]
    c = GLA_CHUNK
    shift = c.bit_length() - 1
    n_chunks = ts // c

    z = _dot_f32(a_ref[...], wa_ref[...]) + ba_ref[...]
    g = _log_sigmoid(z) * (1.0 / GLA_TAU)
    ri = lax.broadcasted_iota(I32, (ts, ts), 0)
    ci = lax.broadcasted_iota(I32, (ts, ts), 1)
    chunk_start = lax.shift_left(lax.shift_right_logical(ri, shift), shift)

    def chunk_causal(val):
        return jnp.where(ci <= ri, jnp.where(ci >= chunk_start, val, 0.0), 0.0)

    incl = chunk_causal(jnp.ones((ts, ts), F32)).astype(BF16)
    hi, mid, lo = _split3(g)
    bc = _dot(incl, hi) + _dot(incl, mid) + _dot(incl, lo)
    decay = []
    for ic in range(n_chunks):
        rows = slice(ic * c, (ic + 1) * c)
        bcc = bc[rows]
        b_mid = bcc[c // 2 - 1:c // 2]
        b_last = bcc[c - 1:c]
        q = q_ref[rows, :].astype(F32)
        k = k_ref[rows, :].astype(F32)
        qe_ref[rows, :] = (q * jnp.exp(bcc)).astype(BF16)
        q2_ref[rows, :] = (q * jnp.exp(bcc - b_mid)).astype(BF16)
        k2_ref[rows, :] = (k * jnp.exp(b_mid - bcc)).astype(BF16)
        kd_ref[rows, :] = (k * jnp.exp(b_last - bcc)).astype(BF16)
        decay.append(jnp.exp(b_last))
    for h in range(GLA_HEADS):
        ks = slice(h * GLA_DKP, (h + 1) * GLA_DKP)
        vs = slice(h * GLA_DV, (h + 1) * GLA_DV)
        att = chunk_causal(_dot_nt(q2_ref[:, ks], k2_ref[:, ks]))
        oi_ref[:, vs] = _dot(att.astype(BF16), v_ref[:, vs])

    for ic in range(n_chunks):
        rows = slice(ic * c, (ic + 1) * c)
        for h in range(GLA_HEADS):
            ks = slice(h * GLA_DKP, (h + 1) * GLA_DKP)
            vs = slice(h * GLA_DV, (h + 1) * GLA_DV)
            st = st_ref[h]
            o = _dot_nt(qe_ref[rows, ks], st.astype(BF16)) + oi_ref[rows, vs]
            upd = lax.dot_general(v_ref[rows, vs], kd_ref[rows, ks], (((0,), (0,)), ((), ())),
                                  preferred_element_type=F32)
            st_ref[h] = st * decay[ic][:, ks] + upd
            on = o * lax.rsqrt(jnp.mean(o * o, axis=-1, keepdims=True) + RMS_EPS) * og_ref[...]
            rh = r_ref[rows, vs].astype(F32)
            o_ref[rows, vs] = (on * rh * jax.nn.sigmoid(rh)).astype(o_ref.dtype)


def _gla(proj, a_side, wa, ba, o_gain, batch, seq):
    ts = GLA_TS
    ns = seq // ts
    kw = GLA_HEADS * GLA_DKP
    vw = GLA_HEADS * GLA_DV
    assert (2 * vw) % kw == 0
    q_blk = (2 * vw) // kw
    return pl.pallas_call(
        _gla_kernel,
        grid=(batch, ns),
        in_specs=[
            pl.BlockSpec((ts, kw), lambda b, s: (b * ns + s, q_blk)),
            pl.BlockSpec((ts, kw), lambda b, s: (b * ns + s, q_blk + 1)),
            pl.BlockSpec((ts, vw), lambda b, s: (b * ns + s, 0)),
            pl.BlockSpec((ts, vw), lambda b, s: (b * ns + s, 1)),
            pl.BlockSpec((ts, LANES), lambda b, s: (b * ns + s, 0)),
            pl.BlockSpec((LANES, kw), lambda b, s: (0, 0)),
            pl.BlockSpec((1, kw), lambda b, s: (0, 0)),
            pl.BlockSpec((1, GLA_DV), lambda b, s: (0, 0)),
        ],
        out_specs=pl.BlockSpec((ts, vw), lambda b, s: (b * ns + s, 0)),
        out_shape=jax.ShapeDtypeStruct((batch * seq, vw), BF16),
        scratch_shapes=[pltpu.VMEM((GLA_HEADS, GLA_DV, GLA_DKP), F32)] + [pltpu.VMEM((ts, kw), BF16)] * 4
        + [pltpu.VMEM((ts, vw), F32)],
        compiler_params=_params("parallel", "arbitrary"),
        name="gla",
    )(proj, proj, proj, proj, a_side, wa, ba, o_gain.reshape(1, GLA_DV))


def _out_proj_kernel(mix_ref, mo_ref, w1_ref, w2_ref, x_ref, g_ref, wr_ref, xo_ref, hn_ref, lg_ref):
    y = _dot(mix_ref[...], w1_ref[...]) + _dot(mo_ref[...], w2_ref[...])
    xn = x_ref[...] + y
    xo_ref[...] = xn
    hn = xn * lax.rsqrt(jnp.mean(xn * xn, axis=-1, keepdims=True) + RMS_EPS) * g_ref[...]
    hn_ref[...] = hn
    h_hi = hn.astype(BF16)
    h_lo = (hn - h_hi.astype(F32)).astype(BF16)
    both = _dot(h_hi, wr_ref[...])
    lg_ref[...] = both[:, :LANES] + both[:, LANES:] + _dot(h_lo, wr_ref[:, :LANES])


def _out_proj(mix, mo, w1, w2, x, g, wr):
    t, d = x.shape
    tm = OUT_TM
    once = pl.Buffered(1)
    return pl.pallas_call(
        _out_proj_kernel,
        grid=(t // tm,),
        in_specs=[
            pl.BlockSpec((tm, mix.shape[1]), lambda i: (i, 0)),
            pl.BlockSpec((tm, mo.shape[1]), lambda i: (i, 0)),
            pl.BlockSpec(w1.shape, lambda i: (0, 0), pipeline_mode=once),
            pl.BlockSpec(w2.shape, lambda i: (0, 0), pipeline_mode=once),
            pl.BlockSpec((tm, d), lambda i: (i, 0)),
            pl.BlockSpec((1, d), lambda i: (0, 0), pipeline_mode=once),
            pl.BlockSpec((d, 2 * LANES), lambda i: (0, 0), pipeline_mode=once),
        ],
        out_specs=[
            pl.BlockSpec((tm, d), lambda i: (i, 0)),
            pl.BlockSpec((tm, d), lambda i: (i, 0)),
            pl.BlockSpec((tm, LANES), lambda i: (i, 0)),
        ],
        out_shape=[jax.ShapeDtypeStruct((t, d), F32), jax.ShapeDtypeStruct((t, d), F32),
                   jax.ShapeDtypeStruct((t, LANES), F32)],
        compiler_params=_params("parallel"),
        name="out_proj",
    )(mix, mo, w1, w2, x, g.reshape(1, d), wr)


def _router_kernel(lg_ref, b_ref, o_ref, cnt_ref, carry_ref):
    @pl.when(pl.program_id(0) == 0)
    def _():
        carry_ref[...] = jnp.zeros_like(carry_ref)

    logits = lg_ref[...].T + b_ref[:, 0:1]
    tt = logits.shape[1]
    gl = [logits[ROUTER_GROUP_ROW + g:ROUTER_GROUP_ROW + g + 1, :] for g in range(N_GROUPS)]
    best = gl[0]
    gidx = jnp.zeros((1, tt), I32)
    for g in range(1, N_GROUPS):
        better = gl[g] > best
        best = jnp.where(better, gl[g], best)
        gidx = jnp.where(better, g, gidx)
    denom = jnp.exp(gl[0] - best)
    for g in range(1, N_GROUPS):
        denom = denom + jnp.exp(gl[g] - best)
    g_p = 1.0 / denom

    epg = EXPERTS_PER_GROUP
    in_grp = logits[ROUTER_EXPERT_ROW:ROUTER_EXPERT_ROW + epg, :]
    for g in range(1, N_GROUPS):
        lo = ROUTER_EXPERT_ROW + g * epg
        in_grp = jnp.where(gidx == g, logits[lo:lo + epg, :], in_grp)
    rows = lax.broadcasted_iota(I32, (epg, tt), 0).astype(F32)
    m1 = jnp.max(in_grp, axis=0, keepdims=True)
    i1 = jnp.min(jnp.where(in_grp == m1, rows, float(epg)), axis=0, keepdims=True)
    rest = jnp.where(rows == i1, -jnp.inf, in_grp)
    m2 = jnp.max(rest, axis=0, keepdims=True)
    i2 = jnp.min(jnp.where(rest == m2, rows, float(epg)), axis=0, keepdims=True)
    e21 = jnp.exp(m2 - m1)
    w1 = g_p / (1.0 + e21)
    w2 = g_p * e21 / (1.0 + e21)
    e1 = gidx * epg + i1.astype(I32)
    e2 = gidx * epg + i2.astype(I32)

    er = lax.broadcasted_iota(I32, (N_EXPERTS, tt), 0)
    oh1 = er == e1
    oh2 = er == e2
    cnt = jnp.where(oh1, 1.0, 0.0) + jnp.where(oh2, 1.0, 0.0)
    r = lax.broadcasted_iota(I32, (tt, tt), 0)
    c = lax.broadcasted_iota(I32, (tt, tt), 1)
    strict = jnp.where(r < c, 1.0, 0.0).astype(BF16)
    before = _dot(cnt.astype(BF16), strict) + carry_ref[:, 0:1]
    rank1 = jnp.sum(jnp.where(oh1, before, 0.0), axis=0, keepdims=True)
    rank2 = jnp.sum(jnp.where(oh2, before, 0.0), axis=0, keepdims=True)
    total = carry_ref[...] + jnp.sum(cnt, axis=1, keepdims=True)
    carry_ref[...] = total
    cnt_ref[...] = total

    o_ref[0:1, :] = e1.astype(F32)
    o_ref[1:2, :] = e2.astype(F32)
    o_ref[2:3, :] = rank1
    o_ref[3:4, :] = rank2
    o_ref[4:5, :] = w1
    o_ref[5:6, :] = w2
    o_ref[6:8, :] = jnp.zeros((2, tt), F32)


def _router(logits, bias_col):
    t = logits.shape[0]
    tt = ROUTER_TT
    return pl.pallas_call(
        _router_kernel,
        grid=(t // tt,),
        in_specs=[
            pl.BlockSpec((tt, LANES), lambda i: (i, 0)),
            pl.BlockSpec((LANES, LANES), lambda i: (0, 0)),
        ],
        out_specs=[
            pl.BlockSpec((SUBLANES, tt), lambda i: (0, i)),
            pl.BlockSpec((N_EXPERTS, LANES), lambda i: (0, 0)),
        ],
        out_shape=[jax.ShapeDtypeStruct((SUBLANES, t), F32), jax.ShapeDtypeStruct((N_EXPERTS, LANES), F32)],
        scratch_shapes=[pltpu.VMEM((N_EXPERTS, LANES), F32)],
        compiler_params=_params("arbitrary"),
        name="router",
    )(logits, bias_col)


def _row_copy(src_ref, src_row, dst_ref, dst_row, sem):
    return pltpu.make_async_copy(src_ref.at[pl.ds(src_row, 1), :], dst_ref.at[pl.ds(dst_row, 1), :], sem)


def _dispatch_kernel(d1_ref, d2_ref, h_ref, xs_in_ref, xs_ref, sem):
    del xs_in_ref
    tm = h_ref.shape[0]
    base = pl.program_id(0) * tm

    def issue(r, carry):
        _row_copy(h_ref, r, xs_ref, d1_ref[base + r], sem).start()
        _row_copy(h_ref, r, xs_ref, d2_ref[base + r], sem).start()
        return carry

    lax.fori_loop(0, tm, issue, 0)

    def drain(r, carry):
        _row_copy(h_ref, r, xs_ref, 0, sem).wait()
        _row_copy(h_ref, r, xs_ref, 0, sem).wait()
        return carry

    lax.fori_loop(0, tm, drain, 0)


def _dispatch(dest1, dest2, hn, p_rows):
    t, d = hn.shape
    tm = ROW_TM
    grid_spec = pltpu.PrefetchScalarGridSpec(
        num_scalar_prefetch=2,
        grid=(t // tm,),
        in_specs=[
            pl.BlockSpec((tm, d), lambda i, a, b: (i, 0)),
            pl.BlockSpec(memory_space=pl.ANY),
        ],
        out_specs=pl.BlockSpec(memory_space=pl.ANY),
        scratch_shapes=[pltpu.SemaphoreType.DMA],
    )
    return pl.pallas_call(
        _dispatch_kernel,
        grid_spec=grid_spec,
        out_shape=jax.ShapeDtypeStruct((p_rows, d), hn.dtype),
        input_output_aliases={3: 0},
        compiler_params=_params("arbitrary"),
        name="moe_dispatch",
    )(dest1, dest2, hn, jnp.zeros((p_rows, d), hn.dtype))


def _moe_kernel(be_ref, valid_ref, x_ref, wg_ref, wu_ref, wd_ref, y_ref):
    b = pl.program_id(0)

    @pl.when(valid_ref[b] != 0)
    def _():
        x = x_ref[...].astype(BF16)
        g = _dot(x, wg_ref[0, 0])
        u = _dot(x, wu_ref[0, 0])
        hid = (g * jax.nn.sigmoid(g) * u).astype(BF16)
        y_ref[...] = _dot(hid, wd_ref[0, 0])

    @pl.when(valid_ref[b] == 0)
    def _():
        y_ref[...] = jnp.zeros_like(y_ref)


def _moe(block_e, block_valid, xs, wg, wu, wd, layer):
    p_rows, d = xs.shape
    tm = MOE_BLOCK
    de = wg.shape[3]
    grid_spec = pltpu.PrefetchScalarGridSpec(
        num_scalar_prefetch=2,
        grid=(p_rows // tm,),
        in_specs=[
            pl.BlockSpec((tm, d), lambda i, be, bv: (i, 0)),
            pl.BlockSpec((1, 1, d, de), lambda i, be, bv: (layer, be[i], 0, 0)),
            pl.BlockSpec((1, 1, d, de), lambda i, be, bv: (layer, be[i], 0, 0)),
            pl.BlockSpec((1, 1, de, d), lambda i, be, bv: (layer, be[i], 0, 0)),
        ],
        out_specs=pl.BlockSpec((tm, d), lambda i, be, bv: (i, 0)),
    )
    return pl.pallas_call(
        _moe_kernel,
        grid_spec=grid_spec,
        out_shape=jax.ShapeDtypeStruct((p_rows, d), F32),
        compiler_params=_params("arbitrary"),
        name="moe_experts",
    )(block_e, block_valid, xs, wg, wu, wd)


def _combine_kernel(d1_ref, d2_ref, x_ref, w_ref, ys_ref, o_ref, buf1, buf2, sem):
    tm = x_ref.shape[0]
    base = pl.program_id(0) * tm

    def issue(r, carry):
        _row_copy(ys_ref, d1_ref[base + r], buf1, r, sem).start()
        _row_copy(ys_ref, d2_ref[base + r], buf2, r, sem).start()
        return carry

    lax.fori_loop(0, tm, issue, 0)

    def drain(r, carry):
        _row_copy(ys_ref, 0, buf1, r, sem).wait()
        _row_copy(ys_ref, 0, buf2, r, sem).wait()
        return carry

    lax.fori_loop(0, tm, drain, 0)
    w = w_ref[...]
    o_ref[...] = x_ref[...] + w[:, 0:1] * buf1[...] + w[:, 1:2] * buf2[...]


def _combine(dest1, dest2, x, w12, ys):
    t, d = x.shape
    tm = ROW_TM
    grid_spec = pltpu.PrefetchScalarGridSpec(
        num_scalar_prefetch=2,
        grid=(t // tm,),
        in_specs=[
            pl.BlockSpec((tm, d), lambda i, a, b: (i, 0)),
            pl.BlockSpec((tm, 2), lambda i, a, b: (i, 0)),
            pl.BlockSpec(memory_space=pl.ANY),
        ],
        out_specs=pl.BlockSpec((tm, d), lambda i, a, b: (i, 0)),
        scratch_shapes=[pltpu.VMEM((tm, d), F32), pltpu.VMEM((tm, d), F32), pltpu.SemaphoreType.DMA],
    )
    return pl.pallas_call(
        _combine_kernel,
        grid_spec=grid_spec,
        out_shape=jax.ShapeDtypeStruct((t, d), F32),
        compiler_params=_params("arbitrary"),
        name="moe_combine",
    )(dest1, dest2, x, w12, ys)


def _pad_cols(w, width):
    return jnp.pad(w, ((0, 0), (0, width - w.shape[1])))


def _moe_layer(x_mid, hn, logits, b_grp, b_exp, w_gate, w_up, w_down, layer):
    t, d = x_mid.shape
    bias = jnp.zeros((LANES,), F32)
    bias = bias.at[ROUTER_GROUP_ROW:ROUTER_GROUP_ROW + N_GROUPS].set(b_grp)
    bias = bias.at[ROUTER_EXPERT_ROW:ROUTER_EXPERT_ROW + N_EXPERTS].set(b_exp)
    routed, totals = _router(logits, jnp.broadcast_to(bias[:, None], (LANES, LANES)))
    e1 = routed[0].astype(I32)
    e2 = routed[1].astype(I32)
    counts = totals[:, 0].astype(I32)
    padded = (counts + MOE_BLOCK - 1) // MOE_BLOCK * MOE_BLOCK
    pad_ends = jnp.cumsum(padded)
    pad_starts = pad_ends - padded
    dest1 = pad_starts[e1] + routed[2].astype(I32)
    dest2 = pad_starts[e2] + routed[3].astype(I32)
    p_rows = 2 * t + N_EXPERTS * MOE_BLOCK
    nb = p_rows // MOE_BLOCK
    block_row = jnp.arange(nb, dtype=I32) * MOE_BLOCK
    block_e = jnp.sum((block_row[:, None] >= pad_ends[None, :]).astype(I32), axis=1)
    block_e = jnp.minimum(block_e, N_EXPERTS - 1)
    block_valid = (block_row < pad_ends[-1]).astype(I32)
    xs = _dispatch(dest1, dest2, hn, p_rows)
    ys = _moe(block_e, block_valid, xs, w_gate, w_up, w_down, layer)
    w12 = jnp.stack([routed[4], routed[5]], axis=-1)
    return _combine(dest1, dest2, x_mid, w12, ys)


def _router_weight(w_grp, w_exp):
    d = w_grp.shape[0]
    wr = jnp.zeros((d, LANES), F32)
    wr = wr.at[:, ROUTER_GROUP_ROW:ROUTER_GROUP_ROW + N_GROUPS].set(w_grp)
    wr = wr.at[:, ROUTER_EXPERT_ROW:ROUTER_EXPERT_ROW + N_EXPERTS].set(w_exp)
    hi = wr.astype(BF16)
    lo = (wr - hi.astype(F32)).astype(BF16)
    return jnp.concatenate([hi, lo], axis=1)


def _fox_layer(x2, attn_g, w_in, b_f, q_g, k_g, mem_kv, memq_g, memk_g, batch, seq, mem_len):
    qkvo = 4 * FOX_W
    w_main = jnp.concatenate([w_in[:, :qkvo], w_in[:, qkvo + FOX_HEADS:]], axis=1).astype(BF16)
    w_gate_cols = _pad_cols(w_in[:, qkvo:qkvo + FOX_HEADS], GATE_GROUP)
    w_side = _pad_cols(jnp.concatenate([w_gate_cols] * 3, axis=1), LANES).astype(BF16)
    scale = LOG2E / math.sqrt(HEAD_DIM)
    head_gain = jnp.concatenate([jnp.tile(q_g * scale, FOX_HEADS), jnp.tile(k_g, FOX_HEADS),
                                 jnp.ones((w_main.shape[1] - 2 * FOX_W,), F32)]).reshape(1, -1)
    proj, f_side = _norm_matmul(x2, attn_g, w_main, head_gain, w_side, n_norm_tiles=2 * FOX_W // PROJ_TN,
                                tm=PROJ_TM, name="fox_in_proj")
    kb = _fox_decay(f_side, b_f, batch, seq)
    mix = _fox_attention(proj, kb, batch, seq)
    mo = _memory_attention(proj, qkvo // MEM_W, mem_kv, memq_g, memk_g, batch, seq, mem_len)
    return mix, mo


def _gla_layer(x2, attn_g, w_in, w_a2, b_a, o_g, mem_kv, memq_g, memk_g, batch, seq, mem_len):
    d = w_in.shape[0]
    kw = GLA_HEADS * GLA_DK
    vw = GLA_HEADS * GLA_DV
    pad = GLA_DKP - GLA_DK

    def pad_heads(w):
        lead = w.shape[0]
        return jnp.pad(w.reshape(lead, GLA_HEADS, GLA_DK), ((0, 0), (0, 0), (0, pad))).reshape(lead, -1)

    wq = pad_heads(w_in[:, :kw] * (1.0 / math.sqrt(GLA_DK)))
    wk = pad_heads(w_in[:, kw:2 * kw])
    rest = w_in[:, 2 * kw:2 * kw + 2 * vw]
    w_a1 = w_in[:, 2 * kw + 2 * vw:2 * kw + 2 * vw + GLA_RANK]
    w_qm = w_in[:, 2 * kw + 2 * vw + GLA_RANK:]
    w_main = jnp.concatenate([rest, wq, wk, w_qm], axis=1).astype(BF16)
    w_side = _pad_cols(w_a1, LANES).astype(BF16)
    head_gain = jnp.ones((1, w_main.shape[1]), F32)
    proj, a_side = _norm_matmul(x2, attn_g, w_main, head_gain, w_side, n_norm_tiles=0, tm=PROJ_TM,
                                name="gla_in_proj")
    wa = jnp.pad(pad_heads(w_a2), ((0, LANES - GLA_RANK), (0, 0)))
    ba = pad_heads(b_a.reshape(1, kw))
    mix = _gla(proj, a_side, wa, ba, o_g, batch, seq)
    q_col_block = (2 * GLA_HEADS * GLA_DKP + 2 * vw) // MEM_W
    mo = _memory_attention(proj, q_col_block, mem_kv, memq_g, memk_g, batch, seq, mem_len)
    return mix, mo


def kernel(x, mem, mem_norm_g, w_mem_kv, attn_norm_g, fox_w_in, fox_b_f, fox_q_g, fox_k_g, gla_w_in, gla_w_a2,
           gla_b_a, gla_o_g, memq_g, memk_g, w_out, ffn_norm_g, w_grp, b_grp, w_exp, b_exp, w_gate, w_up, w_down):
    batch, seq, d = x.shape
    mem_len = mem.shape[1]
    depth = attn_norm_g.shape[0]
    x2 = x.reshape(batch * seq, d)

    n_kv = w_mem_kv.shape[1]
    mem_kv, _ = _norm_matmul(mem.reshape(batch * mem_len, d), mem_norm_g, w_mem_kv.astype(BF16),
                             jnp.ones((1, n_kv), F32), jnp.zeros((d, LANES), BF16), n_norm_tiles=0,
                             tm=batch * mem_len, name="mem_kv_proj")
    w_gate16, w_up16, w_down16 = w_gate.astype(BF16), w_up.astype(BF16), w_down.astype(BF16)

    for i in range(depth):
        j = i // 2
        if i % 2 == 0:
            mix, mo = _fox_layer(x2, attn_norm_g[i], fox_w_in[j], fox_b_f[j], fox_q_g[j], fox_k_g[j],
                                 mem_kv, memq_g[i], memk_g[i], batch, seq, mem_len)
        else:
            mix, mo = _gla_layer(x2, attn_norm_g[i], gla_w_in[j], gla_w_a2[j], gla_b_a[j], gla_o_g[j],
                                 mem_kv, memq_g[i], memk_g[i], batch, seq, mem_len)
        w_o = w_out[i].astype(BF16)
        x_mid, hn, logits = _out_proj(mix, mo, w_o[:mix.shape[1]], w_o[mix.shape[1]:], x2, ffn_norm_g[i],
                                      _router_weight(w_grp[i], w_exp[i]))
        x2 = _moe_layer(x_mid, hn, logits, b_grp[i], b_exp[i], w_gate16, w_up16, w_down16, i)
    return x2.reshape(batch, seq, d)
```

```python
import functools
import math

import jax
import jax.numpy as jnp
from jax import lax
from jax.experimental import pallas as pl
from jax.experimental.pallas import tpu as pltpu

F32 = jnp.float32
BF16 = jnp.bfloat16
I32 = jnp.int32

HEAD_DIM = 128
FOX_HEADS = 12
FOX_W = FOX_HEADS * HEAD_DIM
MEM_HEADS = 4
MEM_W = MEM_HEADS * HEAD_DIM
GLA_HEADS = 4
GLA_DV = FOX_W // GLA_HEADS
GLA_DK = GLA_DV // 2
GLA_DKP = 256
GLA_RANK = 16
GLA_TAU = 16.0
N_GROUPS = 4
EXPERTS_PER_GROUP = 8
N_EXPERTS = N_GROUPS * EXPERTS_PER_GROUP
RMS_EPS = 1e-6
LOG2E = 1.4426950408889634
GATE_GROUP = 16

LANES = 128
SUBLANES = 8
VMEM_LIMIT_BYTES = 56 * 1024 * 1024

PROJ_TM = 1024
PROJ_TN = 512
ATTN_T = 512
ATTN_HEADS_PER_STEP = 6
DECAY_TS = 512
MEM_TQ = 512
OUT_TM = 512
ROUTER_TT = 512
ROW_TM = 256
MOE_BLOCK = 256
ROW_UNROLL = 8
W_CHUNKS = 12
W_PHASE_CHUNKS = 3
GLA_TS = 256
GLA_CHUNK = 64
ROUTER_GROUP_ROW = 0
ROUTER_EXPERT_ROW = 8
NEG_BIG = -1e30


def _params(*sem):
    return pltpu.CompilerParams(dimension_semantics=sem, vmem_limit_bytes=VMEM_LIMIT_BYTES)


def _log_sigmoid(z):
    return jnp.minimum(z, 0.0) - jnp.log1p(jnp.exp(-jnp.abs(z)))


def _dot(a, b):
    return jnp.dot(a, b, preferred_element_type=F32)


def _dot_nt(a, b):
    return lax.dot_general(a, b, (((1,), (1,)), ((), ())), preferred_element_type=F32)


def _dot_f32(a, b):
    return jnp.dot(a, b, preferred_element_type=F32, precision=lax.Precision.HIGHEST)


def _pack_halves(x):
    n = x.shape[1] // 2
    lo = lax.bitcast_convert_type(x[:, :n].astype(BF16).astype(F32), I32)
    hi = lax.bitcast_convert_type(x[:, n:].astype(BF16).astype(F32), I32)
    return lax.shift_right_logical(lo, 16) | hi


def _unpack_halves(w):
    lo = lax.bitcast_convert_type(lax.shift_left(w, 16), F32)
    hi = lax.bitcast_convert_type(w & jnp.int32(-65536), F32)
    return jnp.concatenate([lo.astype(BF16), hi.astype(BF16)], axis=1)


def _split3(x):
    hi = x.astype(BF16)
    r1 = x - hi.astype(F32)
    mid = r1.astype(BF16)
    lo = (r1 - mid.astype(F32)).astype(BF16)
    return hi, mid, lo


def _norm_matmul_kernel(x_ref, g_ref, w_ref, hg_ref, ws_ref, o_ref, os_ref, xn_ref, *, nj_main, n_norm_tiles):
    j = pl.program_id(1)

    @pl.when(j == 0)
    def _():
        x = x_ref[...]
        ms = jnp.mean(x * x, axis=-1, keepdims=True)
        xn_ref[...] = (x * lax.rsqrt(ms + RMS_EPS) * g_ref[...]).astype(BF16)

    if n_norm_tiles > 0:
        @pl.when(j < n_norm_tiles)
        def _():
            acc = _dot(xn_ref[...], w_ref[...])
            for h in range(acc.shape[1] // HEAD_DIM):
                sl = slice(h * HEAD_DIM, (h + 1) * HEAD_DIM)
                a = acc[:, sl]
                ms = jnp.mean(a * a, axis=-1, keepdims=True)
                o_ref[:, sl] = (a * lax.rsqrt(ms + RMS_EPS) * hg_ref[:, sl]).astype(o_ref.dtype)

    @pl.when((j >= n_norm_tiles) & (j < nj_main))
    def _():
        o_ref[...] = _dot(xn_ref[...], w_ref[...]).astype(o_ref.dtype)

    @pl.when(j == nj_main)
    def _():
        os_ref[...] = _dot(xn_ref[...], ws_ref[...])


def _norm_matmul(x, g, w_main, head_gain, w_side, *, n_norm_tiles, tm, name):
    t, d = x.shape
    n_main = w_main.shape[1]
    tn = PROJ_TN
    nj_main = n_main // tn
    assert t % tm == 0 and n_main % tn == 0 and w_side.shape[1] == LANES
    last = nj_main - 1
    kernel = functools.partial(_norm_matmul_kernel, nj_main=nj_main, n_norm_tiles=n_norm_tiles)
    return pl.pallas_call(
        kernel,
        grid=(t // tm, nj_main + 1),
        in_specs=[
            pl.BlockSpec((tm, d), lambda i, j: (i, 0)),
            pl.BlockSpec((1, d), lambda i, j: (0, 0)),
            pl.BlockSpec((d, tn), lambda i, j: (0, jnp.minimum(j, last))),
            pl.BlockSpec((1, tn), lambda i, j: (0, jnp.minimum(j, last))),
            pl.BlockSpec((d, LANES), lambda i, j: (0, 0)),
        ],
        out_specs=[
            pl.BlockSpec((tm, tn), lambda i, j: (i, jnp.minimum(j, last))),
            pl.BlockSpec((tm, LANES), lambda i, j: (i, 0)),
        ],
        out_shape=[jax.ShapeDtypeStruct((t, n_main), BF16), jax.ShapeDtypeStruct((t, LANES), F32)],
        scratch_shapes=[pltpu.VMEM((tm, d), BF16)],
        compiler_params=_params("parallel", "arbitrary"),
        name=name,
    )(x, g.reshape(1, d), w_main, head_gain, w_side)


def _fox_decay_kernel(f_ref, b_ref, kb_ref, carry_ref):
    @pl.when(pl.program_id(1) == 0)
    def _():
        carry_ref[...] = jnp.zeros_like(carry_ref)

    lf = _log_sigmoid(f_ref[...] + b_ref[...]) * (-LOG2E)
    ts = lf.shape[0]
    r = lax.broadcasted_iota(I32, (ts, ts), 0)
    c = lax.broadcasted_iota(I32, (ts, ts), 1)
    incl = jnp.where(c <= r, 1.0, 0.0).astype(BF16)
    hi, mid, lo = _split3(lf)
    cs = _dot(incl, hi) + _dot(incl, mid) + _dot(incl, lo) + carry_ref[0:1, :]
    carry_ref[...] = jnp.broadcast_to(cs[ts - 1:ts, :], carry_ref.shape)
    hi, mid, lo = _split3(cs)
    lane = lax.broadcasted_iota(I32, cs.shape, 1)
    piece = jnp.where(lane < GATE_GROUP, hi.astype(F32),
                      jnp.where(lane < 2 * GATE_GROUP, mid.astype(F32),
                                jnp.where(lane < 3 * GATE_GROUP, lo.astype(F32), 0.0)))
    kb_ref[...] = piece.astype(BF16)


def _fox_decay(f_side, b_f, batch, seq):
    ts = DECAY_TS
    ns = seq // ts
    b_pad = jnp.zeros((LANES,), F32)
    for p in range(3):
        b_pad = b_pad.at[p * GATE_GROUP:p * GATE_GROUP + FOX_HEADS].set(b_f)
    return pl.pallas_call(
        _fox_decay_kernel,
        grid=(batch, ns),
        in_specs=[
            pl.BlockSpec((ts, LANES), lambda b, s: (b * ns + s, 0)),
            pl.BlockSpec((1, LANES), lambda b, s: (0, 0)),
        ],
        out_specs=pl.BlockSpec((ts, LANES), lambda b, s: (b * ns + s, 0)),
        out_shape=jax.ShapeDtypeStruct((batch * seq, LANES), BF16),
        scratch_shapes=[pltpu.VMEM((SUBLANES, LANES), F32)],
        compiler_params=_params("parallel", "arbitrary"),
        name="fox_decay",
    )(f_side, b_pad.reshape(1, LANES))


def _fox_attn_kernel(q_ref, k_ref, v_ref, kb_ref, og_ref, o_ref, m_ref, acc_ref):
    hp = pl.program_id(1)
    qi = pl.program_id(2)
    t = q_ref.shape[0]
    lane = lax.broadcasted_iota(I32, (t, LANES), 1)
    ones = jnp.ones((t, HEAD_DIM), BF16)
    q_aug = []
    for hh in range(ATTN_HEADS_PER_STEP):
        h = hp * ATTN_HEADS_PER_STEP + hh
        own = jnp.where(lane < 3 * GATE_GROUP, jnp.where((lane & (GATE_GROUP - 1)) == h, 1.0, 0.0), 0.0)
        q_aug.append(jnp.concatenate([q_ref[:, hh * HEAD_DIM:(hh + 1) * HEAD_DIM], own.astype(BF16)], axis=1))
    m_ref[...] = jnp.full_like(m_ref, NEG_BIG)
    acc_ref[...] = jnp.zeros_like(acc_ref)

    def tile(j, masked):
        rows = pl.ds(pl.multiple_of(j * t, t), t)
        kb = kb_ref[rows, :]
        for hh in range(ATTN_HEADS_PER_STEP):
            cols = slice(hh * HEAD_DIM, (hh + 1) * HEAD_DIM)
            k_aug = jnp.concatenate([k_ref[rows, cols], kb], axis=1)
            s = _dot_nt(q_aug[hh], k_aug)
            if masked:
                row = lax.broadcasted_iota(I32, (t, t), 0)
                col = lax.broadcasted_iota(I32, (t, t), 1)
                s = jnp.where(col <= row, s, NEG_BIG)
            m_prev = m_ref[hh]
            m_new = jnp.maximum(m_prev, jnp.max(s, axis=-1, keepdims=True))
            alpha = jnp.exp2(m_prev - m_new)
            p = jnp.exp2(s - jnp.concatenate([m_new] * (t // LANES), axis=1))
            v_aug = jnp.concatenate([v_ref[rows, cols], ones], axis=1)
            pv = _dot(p.astype(BF16), v_aug)
            acc_ref[hh] = jnp.concatenate([alpha, alpha], axis=1) * acc_ref[hh] + pv
            m_ref[hh] = m_new

    def full_tile(j, carry):
        tile(j, False)
        return carry

    lax.fori_loop(0, qi, full_tile, 0)
    tile(qi, True)
    for hh in range(ATTN_HEADS_PER_STEP):
        cols = slice(hh * HEAD_DIM, (hh + 1) * HEAD_DIM)
        acc = acc_ref[hh]
        og = og_ref[:, cols].astype(F32)
        o_ref[:, cols] = (acc[:, :HEAD_DIM] / acc[:, HEAD_DIM:] * jax.nn.sigmoid(og)).astype(o_ref.dtype)


def _fox_attention(proj, kb, batch, seq):
    t = ATTN_T
    nq = seq // t
    hps = ATTN_HEADS_PER_STEP
    w = hps * HEAD_DIM
    nhp = FOX_HEADS // hps
    return pl.pallas_call(
        _fox_attn_kernel,
        grid=(batch, nhp, nq),
        in_specs=[
            pl.BlockSpec((t, w), lambda b, h, i: (b * nq + i, h)),
            pl.BlockSpec((seq, w), lambda b, h, i: (b, nhp + h)),
            pl.BlockSpec((seq, w), lambda b, h, i: (b, 2 * nhp + h)),
            pl.BlockSpec((seq, LANES), lambda b, h, i: (b, 0)),
            pl.BlockSpec((t, w), lambda b, h, i: (b * nq + i, 3 * nhp + h)),
        ],
        out_specs=pl.BlockSpec((t, w), lambda b, h, i: (b * nq + i, h)),
        out_shape=jax.ShapeDtypeStruct((batch * seq, FOX_W), BF16),
        scratch_shapes=[pltpu.VMEM((hps, t, LANES), F32), pltpu.VMEM((hps, t, 2 * HEAD_DIM), F32)],
        compiler_params=_params("parallel", "parallel", "arbitrary"),
        name="fox_attention",
    )(proj, proj, proj, kb, proj)


def _mem_attn_kernel(q_ref, k_ref, v_ref, qg_ref, kg_ref, o_ref):
    for h in range(MEM_HEADS):
        sl = slice(h * HEAD_DIM, (h + 1) * HEAD_DIM)
        q = q_ref[:, sl].astype(F32)
        q = q * lax.rsqrt(jnp.mean(q * q, axis=-1, keepdims=True) + RMS_EPS) * qg_ref[...]
        k = k_ref[:, sl].astype(F32)
        k = k * lax.rsqrt(jnp.mean(k * k, axis=-1, keepdims=True) + RMS_EPS) * kg_ref[...]
        s = _dot_nt(q.astype(BF16), k.astype(BF16))
        p = jnp.exp(s - jnp.max(s, axis=-1, keepdims=True))
        o = _dot(p.astype(BF16), v_ref[:, sl]) / jnp.sum(p, axis=-1, keepdims=True)
        o_ref[:, sl] = o.astype(o_ref.dtype)


def _memory_attention(proj, q_col_block, mem_kv, q_gain, k_gain, batch, seq, mem_len):
    tq = MEM_TQ
    nq = seq // tq
    scale = 1.0 / math.sqrt(HEAD_DIM)
    return pl.pallas_call(
        _mem_attn_kernel,
        grid=(batch * nq,),
        in_specs=[
            pl.BlockSpec((tq, MEM_W), lambda i: (i, q_col_block)),
            pl.BlockSpec((mem_len, MEM_W), lambda i: (i // nq, 0)),
            pl.BlockSpec((mem_len, MEM_W), lambda i: (i // nq, 1)),
            pl.BlockSpec((1, HEAD_DIM), lambda i: (0, 0)),
            pl.BlockSpec((1, HEAD_DIM), lambda i: (0, 0)),
        ],
        out_specs=pl.BlockSpec((tq, MEM_W), lambda i: (i, 0)),
        out_shape=jax.ShapeDtypeStruct((batch * seq, MEM_W), BF16),
        compiler_params=_params("parallel"),
        name="memory_attention",
    )(proj, mem_kv, mem_kv, (q_gain * scale).reshape(1, HEAD_DIM), k_gain.reshape(1, HEAD_DIM))


def _gla_kernel(q_ref, k_ref, v_ref, r_ref, a_ref, wa_ref, ba_ref, og_ref, o_ref,
                st_ref, qe_ref, q2_ref, k2_ref, kd_ref, oi_ref):
    @pl.when(pl.program_id(1) == 0)
    def _():
        st_ref[...] = jnp.zeros_like(st_ref)

    ts = q_ref.shape[0]
    c = GLA_CHUNK
    shift = c.bit_length() - 1
    n_chunks = ts // c

    z = _dot_f32(a_ref[...], wa_ref[...]) + ba_ref[...]
    g = _log_sigmoid(z) * (1.0 / GLA_TAU)
    ri = lax.broadcasted_iota(I32, (ts, ts), 0)
    ci = lax.broadcasted_iota(I32, (ts, ts), 1)
    chunk_start = lax.shift_left(lax.shift_right_logical(ri, shift), shift)

    def chunk_causal(val):
        return jnp.where(ci <= ri, jnp.where(ci >= chunk_start, val, 0.0), 0.0)

    incl = chunk_causal(jnp.ones((ts, ts), F32)).astype(BF16)
    hi, mid, lo = _split3(g)
    bc = _dot(incl, hi) + _dot(incl, mid) + _dot(incl, lo)
    decay = []
    for ic in range(n_chunks):
        rows = slice(ic * c, (ic + 1) * c)
        bcc = bc[rows]
        b_mid = bcc[c // 2 - 1:c // 2]
        b_last = bcc[c - 1:c]
        q = q_ref[rows, :].astype(F32)
        k = k_ref[rows, :].astype(F32)
        qe_ref[rows, :] = (q * jnp.exp(bcc)).astype(BF16)
        q2_ref[rows, :] = (q * jnp.exp(bcc - b_mid)).astype(BF16)
        k2_ref[rows, :] = (k * jnp.exp(b_mid - bcc)).astype(BF16)
        kd_ref[rows, :] = (k * jnp.exp(b_last - bcc)).astype(BF16)
        decay.append(jnp.exp(b_last))
    for h in range(GLA_HEADS):
        ks = slice(h * GLA_DKP, (h + 1) * GLA_DKP)
        vs = slice(h * GLA_DV, (h + 1) * GLA_DV)
        att = chunk_causal(_dot_nt(q2_ref[:, ks], k2_ref[:, ks]))
        oi_ref[:, vs] = _dot(att.astype(BF16), v_ref[:, vs])

    for ic in range(n_chunks):
        rows = slice(ic * c, (ic + 1) * c)
        for h in range(GLA_HEADS):
            ks = slice(h * GLA_DKP, (h + 1) * GLA_DKP)
            vs = slice(h * GLA_DV, (h + 1) * GLA_DV)
            st = st_ref[h]
            o = _dot_nt(qe_ref[rows, ks], st.astype(BF16)) + oi_ref[rows, vs]
            upd = lax.dot_general(v_ref[rows, vs], kd_ref[rows, ks], (((0,), (0,)), ((), ())),
                                  preferred_element_type=F32)
            st_ref[h] = st * decay[ic][:, ks] + upd
            on = o * lax.rsqrt(jnp.mean(o * o, axis=-1, keepdims=True) + RMS_EPS) * og_ref[...]
            rh = r_ref[rows, vs].astype(F32)
            o_ref[rows, vs] = (on * rh * jax.nn.sigmoid(rh)).astype(o_ref.dtype)


def _gla(proj, a_side, wa, ba, o_gain, batch, seq):
    ts = GLA_TS
    ns = seq // ts
    kw = GLA_HEADS * GLA_DKP
    vw = GLA_HEADS * GLA_DV
    assert (2 * vw) % kw == 0
    q_blk = (2 * vw) // kw
    return pl.pallas_call(
        _gla_kernel,
        grid=(batch, ns),
        in_specs=[
            pl.BlockSpec((ts, kw), lambda b, s: (b * ns + s, q_blk)),
            pl.BlockSpec((ts, kw), lambda b, s: (b * ns + s, q_blk + 1)),
            pl.BlockSpec((ts, vw), lambda b, s: (b * ns + s, 0)),
            pl.BlockSpec((ts, vw), lambda b, s: (b * ns + s, 1)),
            pl.BlockSpec((ts, LANES), lambda b, s: (b * ns + s, 0)),
            pl.BlockSpec((LANES, kw), lambda b, s: (0, 0)),
            pl.BlockSpec((1, kw), lambda b, s: (0, 0)),
            pl.BlockSpec((1, GLA_DV), lambda b, s: (0, 0)),
        ],
        out_specs=pl.BlockSpec((ts, vw), lambda b, s: (b * ns + s, 0)),
        out_shape=jax.ShapeDtypeStruct((batch * seq, vw), BF16),
        scratch_shapes=[pltpu.VMEM((GLA_HEADS, GLA_DV, GLA_DKP), F32)] + [pltpu.VMEM((ts, kw), BF16)] * 4
        + [pltpu.VMEM((ts, vw), F32)],
        compiler_params=_params("parallel", "arbitrary"),
        name="gla",
    )(proj, proj, proj, proj, a_side, wa, ba, o_gain.reshape(1, GLA_DV))


def _out_proj_kernel(mix_ref, mo_ref, w1_ref, w2_ref, x_ref, g_ref, wr_ref, xo_ref, hn_ref, lg_ref):
    y = _dot(mix_ref[...], w1_ref[...]) + _dot(mo_ref[...], w2_ref[...])
    xn = x_ref[...] + y
    xo_ref[...] = xn
    hn = xn * lax.rsqrt(jnp.mean(xn * xn, axis=-1, keepdims=True) + RMS_EPS) * g_ref[...]
    hn_ref[...] = _pack_halves(hn)
    h_hi = hn.astype(BF16)
    h_lo = (hn - h_hi.astype(F32)).astype(BF16)
    both = _dot(h_hi, wr_ref[...])
    lg_ref[...] = both[:, :LANES] + both[:, LANES:] + _dot(h_lo, wr_ref[:, :LANES])


def _out_proj(mix, mo, w1, w2, x, g, wr):
    t, d = x.shape
    tm = OUT_TM
    once = pl.Buffered(1)
    return pl.pallas_call(
        _out_proj_kernel,
        grid=(t // tm,),
        in_specs=[
            pl.BlockSpec((tm, mix.shape[1]), lambda i: (i, 0)),
            pl.BlockSpec((tm, mo.shape[1]), lambda i: (i, 0)),
            pl.BlockSpec(w1.shape, lambda i: (0, 0), pipeline_mode=once),
            pl.BlockSpec(w2.shape, lambda i: (0, 0), pipeline_mode=once),
            pl.BlockSpec((tm, d), lambda i: (i, 0)),
            pl.BlockSpec((1, d), lambda i: (0, 0), pipeline_mode=once),
            pl.BlockSpec((d, 2 * LANES), lambda i: (0, 0), pipeline_mode=once),
        ],
        out_specs=[
            pl.BlockSpec((tm, d), lambda i: (i, 0)),
            pl.BlockSpec((tm, d // 2), lambda i: (i, 0)),
            pl.BlockSpec((tm, LANES), lambda i: (i, 0)),
        ],
        out_shape=[jax.ShapeDtypeStruct((t, d), F32), jax.ShapeDtypeStruct((t, d // 2), I32),
                   jax.ShapeDtypeStruct((t, LANES), F32)],
        compiler_params=_params("parallel"),
        name="out_proj",
    )(mix, mo, w1, w2, x, g.reshape(1, d), wr)


def _router_kernel(lg_ref, b_ref, o_ref, cnt_ref, carry_ref):
    @pl.when(pl.program_id(0) == 0)
    def _():
        carry_ref[...] = jnp.zeros_like(carry_ref)

    logits = lg_ref[...].T + b_ref[:, 0:1]
    tt = logits.shape[1]
    gl = [logits[ROUTER_GROUP_ROW + g:ROUTER_GROUP_ROW + g + 1, :] for g in range(N_GROUPS)]
    best = gl[0]
    gidx = jnp.zeros((1, tt), I32)
    for g in range(1, N_GROUPS):
        better = gl[g] > best
        best = jnp.where(better, gl[g], best)
        gidx = jnp.where(better, g, gidx)
    denom = jnp.exp(gl[0] - best)
    for g in range(1, N_GROUPS):
        denom = denom + jnp.exp(gl[g] - best)
    g_p = 1.0 / denom

    epg = EXPERTS_PER_GROUP
    in_grp = logits[ROUTER_EXPERT_ROW:ROUTER_EXPERT_ROW + epg, :]
    for g in range(1, N_GROUPS):
        lo = ROUTER_EXPERT_ROW + g * epg
        in_grp = jnp.where(gidx == g, logits[lo:lo + epg, :], in_grp)
    rows = lax.broadcasted_iota(I32, (epg, tt), 0).astype(F32)
    m1 = jnp.max(in_grp, axis=0, keepdims=True)
    i1 = jnp.min(jnp.where(in_grp == m1, rows, float(epg)), axis=0, keepdims=True)
    rest = jnp.where(rows == i1, -jnp.inf, in_grp)
    m2 = jnp.max(rest, axis=0, keepdims=True)
    i2 = jnp.min(jnp.where(rest == m2, rows, float(epg)), axis=0, keepdims=True)
    e21 = jnp.exp(m2 - m1)
    w1 = g_p / (1.0 + e21)
    w2 = g_p * e21 / (1.0 + e21)
    e1 = gidx * epg + i1.astype(I32)
    e2 = gidx * epg + i2.astype(I32)

    er = lax.broadcasted_iota(I32, (N_EXPERTS, tt), 0)
    oh1 = er == e1
    oh2 = er == e2
    cnt = jnp.where(oh1, 1.0, 0.0) + jnp.where(oh2, 1.0, 0.0)
    r = lax.broadcasted_iota(I32, (tt, tt), 0)
    c = lax.broadcasted_iota(I32, (tt, tt), 1)
    strict = jnp.where(r < c, 1.0, 0.0).astype(BF16)
    before = _dot(cnt.astype(BF16), strict) + carry_ref[:, 0:1]
    rank1 = jnp.sum(jnp.where(oh1, before, 0.0), axis=0, keepdims=True)
    rank2 = jnp.sum(jnp.where(oh2, before, 0.0), axis=0, keepdims=True)
    total = carry_ref[...] + jnp.sum(cnt, axis=1, keepdims=True)
    carry_ref[...] = total
    cnt_ref[...] = total

    o_ref[0:1, :] = e1.astype(F32)
    o_ref[1:2, :] = e2.astype(F32)
    o_ref[2:3, :] = rank1
    o_ref[3:4, :] = rank2
    o_ref[4:5, :] = w1
    o_ref[5:6, :] = w2
    o_ref[6:8, :] = jnp.zeros((2, tt), F32)


def _router(logits, bias_col):
    t = logits.shape[0]
    tt = ROUTER_TT
    return pl.pallas_call(
        _router_kernel,
        grid=(t // tt,),
        in_specs=[
            pl.BlockSpec((tt, LANES), lambda i: (i, 0)),
            pl.BlockSpec((LANES, LANES), lambda i: (0, 0)),
        ],
        out_specs=[
            pl.BlockSpec((SUBLANES, tt), lambda i: (0, i)),
            pl.BlockSpec((N_EXPERTS, LANES), lambda i: (0, 0)),
        ],
        out_shape=[jax.ShapeDtypeStruct((SUBLANES, t), F32), jax.ShapeDtypeStruct((N_EXPERTS, LANES), F32)],
        scratch_shapes=[pltpu.VMEM((N_EXPERTS, LANES), F32)],
        compiler_params=_params("arbitrary"),
        name="router",
    )(logits, bias_col)


def _row_copy(src_ref, src_row, dst_ref, dst_row, sem):
    return pltpu.make_async_copy(src_ref.at[pl.ds(src_row, 1), :], dst_ref.at[pl.ds(dst_row, 1), :], sem)


def _dispatch_kernel(d1_ref, d2_ref, h_ref, xs_in_ref, xs_ref, sem):
    del xs_in_ref
    tm = h_ref.shape[0]
    base = pl.program_id(0) * tm

    def issue(g, carry):
        for k in range(ROW_UNROLL):
            r = g * ROW_UNROLL + k
            _row_copy(h_ref, r, xs_ref, d1_ref[base + r], sem).start()
            _row_copy(h_ref, r, xs_ref, d2_ref[base + r], sem).start()
        return carry

    lax.fori_loop(0, tm // ROW_UNROLL, issue, 0)

    def drain(g, carry):
        for k in range(ROW_UNROLL):
            r = g * ROW_UNROLL + k
            _row_copy(h_ref, r, xs_ref, 0, sem).wait()
            _row_copy(h_ref, r, xs_ref, 0, sem).wait()
        return carry

    lax.fori_loop(0, tm // ROW_UNROLL, drain, 0)


def _dispatch(dest1, dest2, hn, p_rows):
    t, d = hn.shape
    tm = ROW_TM
    grid_spec = pltpu.PrefetchScalarGridSpec(
        num_scalar_prefetch=2,
        grid=(t // tm,),
        in_specs=[
            pl.BlockSpec((tm, d), lambda i, a, b: (i, 0)),
            pl.BlockSpec(memory_space=pl.ANY),
        ],
        out_specs=pl.BlockSpec(memory_space=pl.ANY),
        scratch_shapes=[pltpu.SemaphoreType.DMA],
    )
    return pl.pallas_call(
        _dispatch_kernel,
        grid_spec=grid_spec,
        out_shape=jax.ShapeDtypeStruct((p_rows, d), hn.dtype),
        input_output_aliases={3: 0},
        compiler_params=_params("arbitrary"),
        name="moe_dispatch",
    )(dest1, dest2, hn, jnp.zeros((p_rows, d), hn.dtype))


def _weight_chunk(c, d, de):
    per = W_CHUNKS // 3
    if c < 2 * per:
        return c // per, (c % per) * (d // per), 0
    r, h = divmod(c - 2 * per, d // de)
    return 2, r * (d // per), h * de


def _moe_kernel(be_ref, valid_ref, first_ref, ord_ref, slot_ref, nxt_ref, done_ref,
                x_ref, wg_hbm, wu_hbm, wd_hbm, y_ref, wg_c, wu_c, wd_c, stage_ref, sems, *, layer):
    b = pl.program_id(0)
    expert = be_ref[b]
    slot = slot_ref[b]
    ordinal = ord_ref[b]
    nxt = nxt_ref[b]
    hbm = (wg_hbm, wu_hbm, wd_hbm)
    cache = (wg_c, wu_c, wd_c)
    d, de = wg_c.shape[1], wg_c.shape[2]
    rows, cols = stage_ref.shape[1], stage_ref.shape[2]

    def chunk_copy(e, c, k):
        ti, r0, c0 = _weight_chunk(c, d, de)
        src = hbm[ti].at[layer, e, pl.ds(r0, rows), pl.ds(c0, cols)]
        return pltpu.make_async_copy(src, stage_ref.at[k], sems.at[k])

    def start_phase(e, p):
        for k in range(W_PHASE_CHUNKS):
            chunk_copy(e, p * W_PHASE_CHUNKS + k, k).start()

    def finish_phase(e, s, p):
        for k in range(W_PHASE_CHUNKS):
            c = p * W_PHASE_CHUNKS + k
            chunk_copy(e, c, k).wait()
            ti, r0, c0 = _weight_chunk(c, d, de)
            cache[ti][s, pl.ds(r0, rows), pl.ds(c0, cols)] = stage_ref[k].astype(BF16)

    n_phases = W_CHUNKS // W_PHASE_CHUNKS

    @pl.when(valid_ref[b] != 0)
    def _():
        @pl.when(first_ref[b] != 0)
        def _():
            for p in range(n_phases):
                @pl.when(done_ref[b] <= p)
                def _():
                    start_phase(expert, p)
                    finish_phase(expert, slot, p)

        prefetch = (ordinal < n_phases) & (nxt >= 0)
        for p in range(n_phases):
            @pl.when(prefetch & (ordinal == p))
            def _():
                start_phase(nxt, p)

        x = _unpack_halves(x_ref[...])
        g = _dot(x, wg_c[slot])
        u = _dot(x, wu_c[slot])
        hid = (g * jax.nn.sigmoid(g) * u).astype(BF16)
        y_ref[...] = _pack_halves(_dot(hid, wd_c[slot]))

        for p in range(n_phases):
            @pl.when(prefetch & (ordinal == p))
            def _():
                finish_phase(nxt, 1 - slot, p)

    @pl.when(valid_ref[b] == 0)
    def _():
        y_ref[...] = jnp.zeros_like(y_ref)


def _moe(tables, xs, wg, wu, wd, layer):
    p_rows, dh = xs.shape
    tm = MOE_BLOCK
    d, de = wg.shape[2], wg.shape[3]
    assert d == 2 * dh and wd.shape[2:] == (de, d) and d % (W_CHUNKS // 3) == 0 and d % de == 0
    chunk = (d // (W_CHUNKS // 3), de)
    grid_spec = pltpu.PrefetchScalarGridSpec(
        num_scalar_prefetch=len(tables),
        grid=(p_rows // tm,),
        in_specs=[
            pl.BlockSpec((tm, dh), lambda i, *_: (i, 0)),
            pl.BlockSpec(memory_space=pl.ANY),
            pl.BlockSpec(memory_space=pl.ANY),
            pl.BlockSpec(memory_space=pl.ANY),
        ],
        out_specs=pl.BlockSpec((tm, dh), lambda i, *_: (i, 0)),
        scratch_shapes=[
            pltpu.VMEM((2, d, de), BF16), pltpu.VMEM((2, d, de), BF16), pltpu.VMEM((2, de, d), BF16),
            pltpu.VMEM((W_PHASE_CHUNKS,) + chunk, F32), pltpu.SemaphoreType.DMA((W_PHASE_CHUNKS,)),
        ],
    )
    return pl.pallas_call(
        functools.partial(_moe_kernel, layer=layer),
        grid_spec=grid_spec,
        out_shape=jax.ShapeDtypeStruct((p_rows, dh), I32),
        compiler_params=_params("arbitrary"),
        name="moe_experts",
    )(*tables, xs, wg, wu, wd)


def _combine_kernel(d1_ref, d2_ref, x_ref, w_ref, ys_ref, o_ref, buf1, buf2, sem):
    tm = x_ref.shape[0]
    base = pl.program_id(0) * tm

    def issue(g, carry):
        for k in range(ROW_UNROLL):
            r = g * ROW_UNROLL + k
            _row_copy(ys_ref, d1_ref[base + r], buf1, r, sem).start()
            _row_copy(ys_ref, d2_ref[base + r], buf2, r, sem).start()
        return carry

    lax.fori_loop(0, tm // ROW_UNROLL, issue, 0)

    def drain(g, carry):
        for k in range(ROW_UNROLL):
            r = g * ROW_UNROLL + k
            _row_copy(ys_ref, 0, buf1, r, sem).wait()
            _row_copy(ys_ref, 0, buf2, r, sem).wait()
        return carry

    lax.fori_loop(0, tm // ROW_UNROLL, drain, 0)
    w = w_ref[...]
    y1 = _unpack_halves(buf1[...]).astype(F32)
    y2 = _unpack_halves(buf2[...]).astype(F32)
    o_ref[...] = x_ref[...] + w[:, 0:1] * y1 + w[:, 1:2] * y2


def _combine(dest1, dest2, x, w12, ys):
    t, d = x.shape
    tm = ROW_TM
    dh = ys.shape[1]
    grid_spec = pltpu.PrefetchScalarGridSpec(
        num_scalar_prefetch=2,
        grid=(t // tm,),
        in_specs=[
            pl.BlockSpec((tm, d), lambda i, a, b: (i, 0)),
            pl.BlockSpec((tm, 2), lambda i, a, b: (i, 0)),
            pl.BlockSpec(memory_space=pl.ANY),
        ],
        out_specs=pl.BlockSpec((tm, d), lambda i, a, b: (i, 0)),
        scratch_shapes=[pltpu.VMEM((tm, dh), I32), pltpu.VMEM((tm, dh), I32), pltpu.SemaphoreType.DMA],
    )
    return pl.pallas_call(
        _combine_kernel,
        grid_spec=grid_spec,
        out_shape=jax.ShapeDtypeStruct((t, d), F32),
        compiler_params=_params("arbitrary"),
        name="moe_combine",
    )(dest1, dest2, x, w12, ys)


def _pad_cols(w, width):
    return jnp.pad(w, ((0, 0), (0, width - w.shape[1])))


def _moe_layer(x_mid, hn, logits, b_grp, b_exp, w_gate, w_up, w_down, layer):
    t, d = x_mid.shape
    bias = jnp.zeros((LANES,), F32)
    bias = bias.at[ROUTER_GROUP_ROW:ROUTER_GROUP_ROW + N_GROUPS].set(b_grp)
    bias = bias.at[ROUTER_EXPERT_ROW:ROUTER_EXPERT_ROW + N_EXPERTS].set(b_exp)
    routed, totals = _router(logits, jnp.broadcast_to(bias[:, None], (LANES, LANES)))
    e1 = routed[0].astype(I32)
    e2 = routed[1].astype(I32)
    counts = totals[:, 0].astype(I32)
    padded = (counts + MOE_BLOCK - 1) // MOE_BLOCK * MOE_BLOCK
    pad_ends = jnp.cumsum(padded)
    pad_starts = pad_ends - padded
    dest1 = pad_starts[e1] + routed[2].astype(I32)
    dest2 = pad_starts[e2] + routed[3].astype(I32)
    p_rows = 2 * t + N_EXPERTS * MOE_BLOCK
    nb = p_rows // MOE_BLOCK
    block_row = jnp.arange(nb, dtype=I32) * MOE_BLOCK
    block_e = jnp.sum((block_row[:, None] >= pad_ends[None, :]).astype(I32), axis=1)
    block_e = jnp.minimum(block_e, N_EXPERTS - 1)
    block_valid = (block_row < pad_ends[-1]).astype(I32)
    n_phases = W_CHUNKS // W_PHASE_CHUNKS
    eid = jnp.arange(N_EXPERTS, dtype=I32)
    nblk = padded // MOE_BLOCK
    nonempty = nblk > 0
    later = nonempty[None, :] & (eid[None, :] > eid[:, None])
    nxt_e = jnp.min(jnp.where(later, eid[None, :], N_EXPERTS), axis=1)
    nxt_e = jnp.where(nxt_e == N_EXPERTS, -1, nxt_e)
    earlier = nonempty[None, :] & (eid[None, :] < eid[:, None])
    prev_e = jnp.max(jnp.where(earlier, eid[None, :], -1), axis=1)
    done_e = jnp.where(prev_e >= 0, jnp.minimum(nblk[jnp.maximum(prev_e, 0)], n_phases), 0)
    slot_e = (jnp.cumsum(nonempty.astype(I32)) - 1) & 1
    block_ord = (block_row - pad_starts[block_e]) // MOE_BLOCK
    block_first = block_valid * (block_ord == 0).astype(I32)
    tables = (block_e, block_valid, block_first, block_ord.astype(I32), slot_e[block_e].astype(I32),
              nxt_e[block_e].astype(I32), done_e[block_e].astype(I32))
    xs = _dispatch(dest1, dest2, hn, p_rows)
    ys = _moe(tables, xs, w_gate, w_up, w_down, layer)
    w12 = jnp.stack([routed[4], routed[5]], axis=-1)
    return _combine(dest1, dest2, x_mid, w12, ys)


def _router_weight(w_grp, w_exp):
    d = w_grp.shape[0]
    wr = jnp.zeros((d, LANES), F32)
    wr = wr.at[:, ROUTER_GROUP_ROW:ROUTER_GROUP_ROW + N_GROUPS].set(w_grp)
    wr = wr.at[:, ROUTER_EXPERT_ROW:ROUTER_EXPERT_ROW + N_EXPERTS].set(w_exp)
    hi = wr.astype(BF16)
    lo = (wr - hi.astype(F32)).astype(BF16)
    return jnp.concatenate([hi, lo], axis=1)


def _fox_layer(x2, attn_g, w_in, b_f, q_g, k_g, mem_kv, memq_g, memk_g, batch, seq, mem_len):
    qkvo = 4 * FOX_W
    w_main = jnp.concatenate([w_in[:, :qkvo], w_in[:, qkvo + FOX_HEADS:]], axis=1).astype(BF16)
    w_gate_cols = _pad_cols(w_in[:, qkvo:qkvo + FOX_HEADS], GATE_GROUP)
    w_side = _pad_cols(jnp.concatenate([w_gate_cols] * 3, axis=1), LANES).astype(BF16)
    scale = LOG2E / math.sqrt(HEAD_DIM)
    head_gain = jnp.concatenate([jnp.tile(q_g * scale, FOX_HEADS), jnp.tile(k_g, FOX_HEADS),
                                 jnp.ones((w_main.shape[1] - 2 * FOX_W,), F32)]).reshape(1, -1)
    proj, f_side = _norm_matmul(x2, attn_g, w_main, head_gain, w_side, n_norm_tiles=2 * FOX_W // PROJ_TN,
                                tm=PROJ_TM, name="fox_in_proj")
    kb = _fox_decay(f_side, b_f, batch, seq)
    mix = _fox_attention(proj, kb, batch, seq)
    mo = _memory_attention(proj, qkvo // MEM_W, mem_kv, memq_g, memk_g, batch, seq, mem_len)
    return mix, mo


def _gla_layer(x2, attn_g, w_in, w_a2, b_a, o_g, mem_kv, memq_g, memk_g, batch, seq, mem_len):
    d = w_in.shape[0]
    kw = GLA_HEADS * GLA_DK
    vw = GLA_HEADS * GLA_DV
    pad = GLA_DKP - GLA_DK

    def pad_heads(w):
        lead = w.shape[0]
        return jnp.pad(w.reshape(lead, GLA_HEADS, GLA_DK), ((0, 0), (0, 0), (0, pad))).reshape(lead, -1)

    wq = pad_heads(w_in[:, :kw] * (1.0 / math.sqrt(GLA_DK)))
    wk = pad_heads(w_in[:, kw:2 * kw])
    rest = w_in[:, 2 * kw:2 * kw + 2 * vw]
    w_a1 = w_in[:, 2 * kw + 2 * vw:2 * kw + 2 * vw + GLA_RANK]
    w_qm = w_in[:, 2 * kw + 2 * vw + GLA_RANK:]
    w_main = jnp.concatenate([rest, wq, wk, w_qm], axis=1).astype(BF16)
    w_side = _pad_cols(w_a1, LANES).astype(BF16)
    head_gain = jnp.ones((1, w_main.shape[1]), F32)
    proj, a_side = _norm_matmul(x2, attn_g, w_main, head_gain, w_side, n_norm_tiles=0, tm=PROJ_TM,
                                name="gla_in_proj")
    wa = jnp.pad(pad_heads(w_a2), ((0, LANES - GLA_RANK), (0, 0)))
    ba = pad_heads(b_a.reshape(1, kw))
    mix = _gla(proj, a_side, wa, ba, o_g, batch, seq)
    q_col_block = (2 * GLA_HEADS * GLA_DKP + 2 * vw) // MEM_W
    mo = _memory_attention(proj, q_col_block, mem_kv, memq_g, memk_g, batch, seq, mem_len)
    return mix, mo


def kernel(x, mem, mem_norm_g, w_mem_kv, attn_norm_g, fox_w_in, fox_b_f, fox_q_g, fox_k_g, gla_w_in, gla_w_a2,
           gla_b_a, gla_o_g, memq_g, memk_g, w_out, ffn_norm_g, w_grp, b_grp, w_exp, b_exp, w_gate, w_up, w_down):
    batch, seq, d = x.shape
    mem_len = mem.shape[1]
    depth = attn_norm_g.shape[0]
    x2 = x.reshape(batch * seq, d)

    n_kv = w_mem_kv.shape[1]
    mem_kv, _ = _norm_matmul(mem.reshape(batch * mem_len, d), mem_norm_g, w_mem_kv.astype(BF16),
                             jnp.ones((1, n_kv), F32), jnp.zeros((d, LANES), BF16), n_norm_tiles=0,
                             tm=batch * mem_len, name="mem_kv_proj")

    for i in range(depth):
        j = i // 2
        if i % 2 == 0:
            mix, mo = _fox_layer(x2, attn_norm_g[i], fox_w_in[j], fox_b_f[j], fox_q_g[j], fox_k_g[j],
                                 mem_kv, memq_g[i], memk_g[i], batch, seq, mem_len)
        else:
            mix, mo = _gla_layer(x2, attn_norm_g[i], gla_w_in[j], gla_w_a2[j], gla_b_a[j], gla_o_g[j],
                                 mem_kv, memq_g[i], memk_g[i], batch, seq, mem_len)
        w_o = w_out[i].astype(BF16)
        x_mid, hn, logits = _out_proj(mix, mo, w_o[:mix.shape[1]], w_o[mix.shape[1]:], x2, ffn_norm_g[i],
                                      _router_weight(w_grp[i], w_exp[i]))
        x2 = _moe_layer(x_mid, hn, logits, b_grp[i], b_exp[i], w_gate, w_up, w_down, i)
    return x2.reshape(batch, seq, d)
```

```python
import functools
import math

import jax
import jax.numpy as jnp
from jax import lax
from jax.experimental import pallas as pl
from jax.experimental.pallas import tpu as pltpu

F32 = jnp.float32
BF16 = jnp.bfloat16
I32 = jnp.int32

HEAD_DIM = 128
FOX_HEADS = 12
FOX_W = FOX_HEADS * HEAD_DIM
MEM_HEADS = 4
MEM_W = MEM_HEADS * HEAD_DIM
GLA_HEADS = 4
GLA_DV = FOX_W // GLA_HEADS
GLA_DK = GLA_DV // 2
GLA_DKP = 256
GLA_RANK = 16
GLA_TAU = 16.0
N_GROUPS = 4
EXPERTS_PER_GROUP = 8
N_EXPERTS = N_GROUPS * EXPERTS_PER_GROUP
RMS_EPS = 1e-6
LOG2E = 1.4426950408889634
GATE_GROUP = 16

LANES = 128
SUBLANES = 8
VMEM_LIMIT_BYTES = 56 * 1024 * 1024

PROJ_TM = 1024
PROJ_TN = 512
ATTN_T = 512
ATTN_HEADS_PER_STEP = 6
DECAY_TS = 512
MEM_TQ = 512
OUT_TM = 512
ROUTER_TT = 512
ROW_TM = 256
MOE_BLOCK = 256
ROW_UNROLL = 8
W_CHUNKS = 12
W_PHASE_CHUNKS = 3
W_CHUNK_SPLIT = 4
GLA_TS = 256
GLA_CHUNK = 64
ROUTER_GROUP_ROW = 0
ROUTER_EXPERT_ROW = 8
NEG_BIG = -1e30


def _params(*sem):
    return pltpu.CompilerParams(dimension_semantics=sem, vmem_limit_bytes=VMEM_LIMIT_BYTES)


def _log_sigmoid(z):
    return jnp.minimum(z, 0.0) - jnp.log1p(jnp.exp(-jnp.abs(z)))


def _dot(a, b):
    return jnp.dot(a, b, preferred_element_type=F32)


def _dot_nt(a, b):
    return lax.dot_general(a, b, (((1,), (1,)), ((), ())), preferred_element_type=F32)


def _dot_f32(a, b):
    return jnp.dot(a, b, preferred_element_type=F32, precision=lax.Precision.HIGHEST)


def _pack_halves(x):
    n = x.shape[1] // 2
    lo = lax.bitcast_convert_type(x[:, :n].astype(BF16).astype(F32), I32)
    hi = lax.bitcast_convert_type(x[:, n:].astype(BF16).astype(F32), I32)
    return lax.shift_right_logical(lo, 16) | hi


def _unpack_halves(w):
    lo = lax.bitcast_convert_type(lax.shift_left(w, 16), F32)
    hi = lax.bitcast_convert_type(w & jnp.int32(-65536), F32)
    return jnp.concatenate([lo.astype(BF16), hi.astype(BF16)], axis=1)


def _split3(x):
    hi = x.astype(BF16)
    r1 = x - hi.astype(F32)
    mid = r1.astype(BF16)
    lo = (r1 - mid.astype(F32)).astype(BF16)
    return hi, mid, lo


def _norm_matmul_kernel(x_ref, g_ref, w_ref, hg_ref, ws_ref, o_ref, os_ref, xn_ref, *, nj_main, n_norm_tiles):
    j = pl.program_id(1)

    @pl.when(j == 0)
    def _():
        x = x_ref[...]
        ms = jnp.mean(x * x, axis=-1, keepdims=True)
        xn_ref[...] = (x * lax.rsqrt(ms + RMS_EPS) * g_ref[...]).astype(BF16)

    if n_norm_tiles > 0:
        @pl.when(j < n_norm_tiles)
        def _():
            acc = _dot(xn_ref[...], w_ref[...])
            for h in range(acc.shape[1] // HEAD_DIM):
                sl = slice(h * HEAD_DIM, (h + 1) * HEAD_DIM)
                a = acc[:, sl]
                ms = jnp.mean(a * a, axis=-1, keepdims=True)
                o_ref[:, sl] = (a * lax.rsqrt(ms + RMS_EPS) * hg_ref[:, sl]).astype(o_ref.dtype)

    @pl.when((j >= n_norm_tiles) & (j < nj_main))
    def _():
        o_ref[...] = _dot(xn_ref[...], w_ref[...]).astype(o_ref.dtype)

    @pl.when(j == nj_main)
    def _():
        os_ref[...] = _dot(xn_ref[...], ws_ref[...])


def _norm_matmul(x, g, w_main, head_gain, w_side, *, n_norm_tiles, tm, name):
    t, d = x.shape
    n_main = w_main.shape[1]
    tn = PROJ_TN
    nj_main = n_main // tn
    assert t % tm == 0 and n_main % tn == 0 and w_side.shape[1] == LANES
    last = nj_main - 1
    kernel = functools.partial(_norm_matmul_kernel, nj_main=nj_main, n_norm_tiles=n_norm_tiles)
    return pl.pallas_call(
        kernel,
        grid=(t // tm, nj_main + 1),
        in_specs=[
            pl.BlockSpec((tm, d), lambda i, j: (i, 0)),
            pl.BlockSpec((1, d), lambda i, j: (0, 0)),
            pl.BlockSpec((d, tn), lambda i, j: (0, jnp.minimum(j, last))),
            pl.BlockSpec((1, tn), lambda i, j: (0, jnp.minimum(j, last))),
            pl.BlockSpec((d, LANES), lambda i, j: (0, 0)),
        ],
        out_specs=[
            pl.BlockSpec((tm, tn), lambda i, j: (i, jnp.minimum(j, last))),
            pl.BlockSpec((tm, LANES), lambda i, j: (i, 0)),
        ],
        out_shape=[jax.ShapeDtypeStruct((t, n_main), BF16), jax.ShapeDtypeStruct((t, LANES), F32)],
        scratch_shapes=[pltpu.VMEM((tm, d), BF16)],
        compiler_params=_params("parallel", "arbitrary"),
        name=name,
    )(x, g.reshape(1, d), w_main, head_gain, w_side)


def _fox_decay_kernel(f_ref, b_ref, kb_ref, carry_ref):
    @pl.when(pl.program_id(1) == 0)
    def _():
        carry_ref[...] = jnp.zeros_like(carry_ref)

    lf = _log_sigmoid(f_ref[...] + b_ref[...]) * (-LOG2E)
    ts = lf.shape[0]
    r = lax.broadcasted_iota(I32, (ts, ts), 0)
    c = lax.broadcasted_iota(I32, (ts, ts), 1)
    incl = jnp.where(c <= r, 1.0, 0.0).astype(BF16)
    hi, mid, lo = _split3(lf)
    cs = _dot(incl, hi) + _dot(incl, mid) + _dot(incl, lo) + carry_ref[0:1, :]
    carry_ref[...] = jnp.broadcast_to(cs[ts - 1:ts, :], carry_ref.shape)
    hi, mid, lo = _split3(cs)
    lane = lax.broadcasted_iota(I32, cs.shape, 1)
    piece = jnp.where(lane < GATE_GROUP, hi.astype(F32),
                      jnp.where(lane < 2 * GATE_GROUP, mid.astype(F32),
                                jnp.where(lane < 3 * GATE_GROUP, lo.astype(F32), 0.0)))
    kb_ref[...] = piece.astype(BF16)


def _fox_decay(f_side, b_f, batch, seq):
    ts = DECAY_TS
    ns = seq // ts
    b_pad = jnp.zeros((LANES,), F32)
    for p in range(3):
        b_pad = b_pad.at[p * GATE_GROUP:p * GATE_GROUP + FOX_HEADS].set(b_f)
    return pl.pallas_call(
        _fox_decay_kernel,
        grid=(batch, ns),
        in_specs=[
            pl.BlockSpec((ts, LANES), lambda b, s: (b * ns + s, 0)),
            pl.BlockSpec((1, LANES), lambda b, s: (0, 0)),
        ],
        out_specs=pl.BlockSpec((ts, LANES), lambda b, s: (b * ns + s, 0)),
        out_shape=jax.ShapeDtypeStruct((batch * seq, LANES), BF16),
        scratch_shapes=[pltpu.VMEM((SUBLANES, LANES), F32)],
        compiler_params=_params("parallel", "arbitrary"),
        name="fox_decay",
    )(f_side, b_pad.reshape(1, LANES))


def _fox_attn_kernel(q_ref, k_ref, v_ref, kb_ref, og_ref, o_ref, m_ref, acc_ref):
    hp = pl.program_id(1)
    qi = pl.program_id(2)
    t = q_ref.shape[0]
    lane = lax.broadcasted_iota(I32, (t, LANES), 1)
    ones = jnp.ones((t, HEAD_DIM), BF16)
    q_aug = []
    for hh in range(ATTN_HEADS_PER_STEP):
        h = hp * ATTN_HEADS_PER_STEP + hh
        own = jnp.where(lane < 3 * GATE_GROUP, jnp.where((lane & (GATE_GROUP - 1)) == h, 1.0, 0.0), 0.0)
        q_aug.append(jnp.concatenate([q_ref[:, hh * HEAD_DIM:(hh + 1) * HEAD_DIM], own.astype(BF16)], axis=1))
    m_ref[...] = jnp.full_like(m_ref, NEG_BIG)
    acc_ref[...] = jnp.zeros_like(acc_ref)

    def tile(j, masked):
        rows = pl.ds(pl.multiple_of(j * t, t), t)
        kb = kb_ref[rows, :]
        for hh in range(ATTN_HEADS_PER_STEP):
            cols = slice(hh * HEAD_DIM, (hh + 1) * HEAD_DIM)
            k_aug = jnp.concatenate([k_ref[rows, cols], kb], axis=1)
            s = _dot_nt(q_aug[hh], k_aug)
            if masked:
                row = lax.broadcasted_iota(I32, (t, t), 0)
                col = lax.broadcasted_iota(I32, (t, t), 1)
                s = jnp.where(col <= row, s, NEG_BIG)
            m_prev = m_ref[hh]
            m_new = jnp.maximum(m_prev, jnp.max(s, axis=-1, keepdims=True))
            alpha = jnp.exp2(m_prev - m_new)
            p = jnp.exp2(s - jnp.concatenate([m_new] * (t // LANES), axis=1))
            v_aug = jnp.concatenate([v_ref[rows, cols], ones], axis=1)
            pv = _dot(p.astype(BF16), v_aug)
            acc_ref[hh] = jnp.concatenate([alpha, alpha], axis=1) * acc_ref[hh] + pv
            m_ref[hh] = m_new

    def full_tile(j, carry):
        tile(j, False)
        return carry

    lax.fori_loop(0, qi, full_tile, 0)
    tile(qi, True)
    for hh in range(ATTN_HEADS_PER_STEP):
        cols = slice(hh * HEAD_DIM, (hh + 1) * HEAD_DIM)
        acc = acc_ref[hh]
        og = og_ref[:, cols].astype(F32)
        o_ref[:, cols] = (acc[:, :HEAD_DIM] / acc[:, HEAD_DIM:] * jax.nn.sigmoid(og)).astype(o_ref.dtype)


def _fox_attention(proj, kb, batch, seq):
    t = ATTN_T
    nq = seq // t
    hps = ATTN_HEADS_PER_STEP
    w = hps * HEAD_DIM
    nhp = FOX_HEADS // hps
    return pl.pallas_call(
        _fox_attn_kernel,
        grid=(batch, nhp, nq),
        in_specs=[
            pl.BlockSpec((t, w), lambda b, h, i: (b * nq + i, h)),
            pl.BlockSpec((seq, w), lambda b, h, i: (b, nhp + h)),
            pl.BlockSpec((seq, w), lambda b, h, i: (b, 2 * nhp + h)),
            pl.BlockSpec((seq, LANES), lambda b, h, i: (b, 0)),
            pl.BlockSpec((t, w), lambda b, h, i: (b * nq + i, 3 * nhp + h)),
        ],
        out_specs=pl.BlockSpec((t, w), lambda b, h, i: (b * nq + i, h)),
        out_shape=jax.ShapeDtypeStruct((batch * seq, FOX_W), BF16),
        scratch_shapes=[pltpu.VMEM((hps, t, LANES), F32), pltpu.VMEM((hps, t, 2 * HEAD_DIM), F32)],
        compiler_params=_params("parallel", "parallel", "arbitrary"),
        name="fox_attention",
    )(proj, proj, proj, kb, proj)


def _mem_attn_kernel(q_ref, k_ref, v_ref, qg_ref, kg_ref, o_ref):
    for h in range(MEM_HEADS):
        sl = slice(h * HEAD_DIM, (h + 1) * HEAD_DIM)
        q = q_ref[:, sl].astype(F32)
        q = q * lax.rsqrt(jnp.mean(q * q, axis=-1, keepdims=True) + RMS_EPS) * qg_ref[...]
        k = k_ref[:, sl].astype(F32)
        k = k * lax.rsqrt(jnp.mean(k * k, axis=-1, keepdims=True) + RMS_EPS) * kg_ref[...]
        s = _dot_nt(q.astype(BF16), k.astype(BF16))
        p = jnp.exp(s - jnp.max(s, axis=-1, keepdims=True))
        o = _dot(p.astype(BF16), v_ref[:, sl]) / jnp.sum(p, axis=-1, keepdims=True)
        o_ref[:, sl] = o.astype(o_ref.dtype)


def _memory_attention(proj, q_col_block, mem_kv, q_gain, k_gain, batch, seq, mem_len):
    tq = MEM_TQ
    nq = seq // tq
    scale = 1.0 / math.sqrt(HEAD_DIM)
    return pl.pallas_call(
        _mem_attn_kernel,
        grid=(batch * nq,),
        in_specs=[
            pl.BlockSpec((tq, MEM_W), lambda i: (i, q_col_block)),
            pl.BlockSpec((mem_len, MEM_W), lambda i: (i // nq, 0)),
            pl.BlockSpec((mem_len, MEM_W), lambda i: (i // nq, 1)),
            pl.BlockSpec((1, HEAD_DIM), lambda i: (0, 0)),
            pl.BlockSpec((1, HEAD_DIM), lambda i: (0, 0)),
        ],
        out_specs=pl.BlockSpec((tq, MEM_W), lambda i: (i, 0)),
        out_shape=jax.ShapeDtypeStruct((batch * seq, MEM_W), BF16),
        compiler_params=_params("parallel"),
        name="memory_attention",
    )(proj, mem_kv, mem_kv, (q_gain * scale).reshape(1, HEAD_DIM), k_gain.reshape(1, HEAD_DIM))


def _gla_kernel(q_ref, k_ref, v_ref, r_ref, a_ref, wa_ref, ba_ref, og_ref, o_ref,
                st_ref, qe_ref, q2_ref, k2_ref, kd_ref, oi_ref):
    @pl.when(pl.program_id(1) == 0)
    def _():
        st_ref[...] = jnp.zeros_like(st_ref)

    ts = q_ref.shape[0]
    c = GLA_CHUNK
    shift = c.bit_length() - 1
    n_chunks = ts // c

    z = _dot_f32(a_ref[...], wa_ref[...]) + ba_ref[...]
    g = _log_sigmoid(z) * (1.0 / GLA_TAU)
    ri = lax.broadcasted_iota(I32, (ts, ts), 0)
    ci = lax.broadcasted_iota(I32, (ts, ts), 1)
    chunk_start = lax.shift_left(lax.shift_right_logical(ri, shift), shift)

    def chunk_causal(val):
        return jnp.where(ci <= ri, jnp.where(ci >= chunk_start, val, 0.0), 0.0)

    incl = chunk_causal(jnp.ones((ts, ts), F32)).astype(BF16)
    hi, mid, lo = _split3(g)
    bc = _dot(incl, hi) + _dot(incl, mid) + _dot(incl, lo)
    decay = []
    for ic in range(n_chunks):
        rows = slice(ic * c, (ic + 1) * c)
        bcc = bc[rows]
        b_mid = bcc[c // 2 - 1:c // 2]
        b_last = bcc[c - 1:c]
        q = q_ref[rows, :].astype(F32)
        k = k_ref[rows, :].astype(F32)
        qe_ref[rows, :] = (q * jnp.exp(bcc)).astype(BF16)
        q2_ref[rows, :] = (q * jnp.exp(bcc - b_mid)).astype(BF16)
        k2_ref[rows, :] = (k * jnp.exp(b_mid - bcc)).astype(BF16)
        kd_ref[rows, :] = (k * jnp.exp(b_last - bcc)).astype(BF16)
        decay.append(jnp.exp(b_last))
    for h in range(GLA_HEADS):
        ks = slice(h * GLA_DKP, (h + 1) * GLA_DKP)
        vs = slice(h * GLA_DV, (h + 1) * GLA_DV)
        att = chunk_causal(_dot_nt(q2_ref[:, ks], k2_ref[:, ks]))
        oi_ref[:, vs] = _dot(att.astype(BF16), v_ref[:, vs])

    for ic in range(n_chunks):
        rows = slice(ic * c, (ic + 1) * c)
        for h in range(GLA_HEADS):
            ks = slice(h * GLA_DKP, (h + 1) * GLA_DKP)
            vs = slice(h * GLA_DV, (h + 1) * GLA_DV)
            st = st_ref[h]
            o = _dot_nt(qe_ref[rows, ks], st.astype(BF16)) + oi_ref[rows, vs]
            upd = lax.dot_general(v_ref[rows, vs], kd_ref[rows, ks], (((0,), (0,)), ((), ())),
                                  preferred_element_type=F32)
            st_ref[h] = st * decay[ic][:, ks] + upd
            on = o * lax.rsqrt(jnp.mean(o * o, axis=-1, keepdims=True) + RMS_EPS) * og_ref[...]
            rh = r_ref[rows, vs].astype(F32)
            o_ref[rows, vs] = (on * rh * jax.nn.sigmoid(rh)).astype(o_ref.dtype)


def _gla(proj, a_side, wa, ba, o_gain, batch, seq):
    ts = GLA_TS
    ns = seq // ts
    kw = GLA_HEADS * GLA_DKP
    vw = GLA_HEADS * GLA_DV
    assert (2 * vw) % kw == 0
    q_blk = (2 * vw) // kw
    return pl.pallas_call(
        _gla_kernel,
        grid=(batch, ns),
        in_specs=[
            pl.BlockSpec((ts, kw), lambda b, s: (b * ns + s, q_blk)),
            pl.BlockSpec((ts, kw), lambda b, s: (b * ns + s, q_blk + 1)),
            pl.BlockSpec((ts, vw), lambda b, s: (b * ns + s, 0)),
            pl.BlockSpec((ts, vw), lambda b, s: (b * ns + s, 1)),
            pl.BlockSpec((ts, LANES), lambda b, s: (b * ns + s, 0)),
            pl.BlockSpec((LANES, kw), lambda b, s: (0, 0)),
            pl.BlockSpec((1, kw), lambda b, s: (0, 0)),
            pl.BlockSpec((1, GLA_DV), lambda b, s: (0, 0)),
        ],
        out_specs=pl.BlockSpec((ts, vw), lambda b, s: (b * ns + s, 0)),
        out_shape=jax.ShapeDtypeStruct((batch * seq, vw), BF16),
        scratch_shapes=[pltpu.VMEM((GLA_HEADS, GLA_DV, GLA_DKP), F32)] + [pltpu.VMEM((ts, kw), BF16)] * 4
        + [pltpu.VMEM((ts, vw), F32)],
        compiler_params=_params("parallel", "arbitrary"),
        name="gla",
    )(proj, proj, proj, proj, a_side, wa, ba, o_gain.reshape(1, GLA_DV))


def _out_proj_kernel(mix_ref, mo_ref, w1_ref, w2_ref, x_ref, g_ref, wr_ref, xo_ref, hn_ref, lg_ref):
    y = _dot(mix_ref[...], w1_ref[...]) + _dot(mo_ref[...], w2_ref[...])
    xn = x_ref[...] + y
    xo_ref[...] = xn
    hn = xn * lax.rsqrt(jnp.mean(xn * xn, axis=-1, keepdims=True) + RMS_EPS) * g_ref[...]
    hn_ref[...] = _pack_halves(hn)
    h_hi = hn.astype(BF16)
    h_lo = (hn - h_hi.astype(F32)).astype(BF16)
    both = _dot(h_hi, wr_ref[...])
    lg_ref[...] = both[:, :LANES] + both[:, LANES:] + _dot(h_lo, wr_ref[:, :LANES])


def _out_proj(mix, mo, w1, w2, x, g, wr):
    t, d = x.shape
    tm = OUT_TM
    once = pl.Buffered(1)
    return pl.pallas_call(
        _out_proj_kernel,
        grid=(t // tm,),
        in_specs=[
            pl.BlockSpec((tm, mix.shape[1]), lambda i: (i, 0)),
            pl.BlockSpec((tm, mo.shape[1]), lambda i: (i, 0)),
            pl.BlockSpec(w1.shape, lambda i: (0, 0), pipeline_mode=once),
            pl.BlockSpec(w2.shape, lambda i: (0, 0), pipeline_mode=once),
            pl.BlockSpec((tm, d), lambda i: (i, 0)),
            pl.BlockSpec((1, d), lambda i: (0, 0), pipeline_mode=once),
            pl.BlockSpec((d, 2 * LANES), lambda i: (0, 0), pipeline_mode=once),
        ],
        out_specs=[
            pl.BlockSpec((tm, d), lambda i: (i, 0)),
            pl.BlockSpec((tm, d // 2), lambda i: (i, 0)),
            pl.BlockSpec((tm, LANES), lambda i: (i, 0)),
        ],
        out_shape=[jax.ShapeDtypeStruct((t, d), F32), jax.ShapeDtypeStruct((t, d // 2), I32),
                   jax.ShapeDtypeStruct((t, LANES), F32)],
        compiler_params=_params("parallel"),
        name="out_proj",
    )(mix, mo, w1, w2, x, g.reshape(1, d), wr)


def _router_kernel(lg_ref, b_ref, o_ref, cnt_ref, carry_ref):
    @pl.when(pl.program_id(0) == 0)
    def _():
        carry_ref[...] = jnp.zeros_like(carry_ref)

    logits = lg_ref[...].T + b_ref[:, 0:1]
    tt = logits.shape[1]
    gl = [logits[ROUTER_GROUP_ROW + g:ROUTER_GROUP_ROW + g + 1, :] for g in range(N_GROUPS)]
    best = gl[0]
    gidx = jnp.zeros((1, tt), I32)
    for g in range(1, N_GROUPS):
        better = gl[g] > best
        best = jnp.where(better, gl[g], best)
        gidx = jnp.where(better, g, gidx)
    denom = jnp.exp(gl[0] - best)
    for g in range(1, N_GROUPS):
        denom = denom + jnp.exp(gl[g] - best)
    g_p = 1.0 / denom

    epg = EXPERTS_PER_GROUP
    in_grp = logits[ROUTER_EXPERT_ROW:ROUTER_EXPERT_ROW + epg, :]
    for g in range(1, N_GROUPS):
        lo = ROUTER_EXPERT_ROW + g * epg
        in_grp = jnp.where(gidx == g, logits[lo:lo + epg, :], in_grp)
    rows = lax.broadcasted_iota(I32, (epg, tt), 0).astype(F32)
    m1 = jnp.max(in_grp, axis=0, keepdims=True)
    i1 = jnp.min(jnp.where(in_grp == m1, rows, float(epg)), axis=0, keepdims=True)
    rest = jnp.where(rows == i1, -jnp.inf, in_grp)
    m2 = jnp.max(rest, axis=0, keepdims=True)
    i2 = jnp.min(jnp.where(rest == m2, rows, float(epg)), axis=0, keepdims=True)
    e21 = jnp.exp(m2 - m1)
    w1 = g_p / (1.0 + e21)
    w2 = g_p * e21 / (1.0 + e21)
    e1 = gidx * epg + i1.astype(I32)
    e2 = gidx * epg + i2.astype(I32)

    er = lax.broadcasted_iota(I32, (N_EXPERTS, tt), 0)
    oh1 = er == e1
    oh2 = er == e2
    cnt = jnp.where(oh1, 1.0, 0.0) + jnp.where(oh2, 1.0, 0.0)
    r = lax.broadcasted_iota(I32, (tt, tt), 0)
    c = lax.broadcasted_iota(I32, (tt, tt), 1)
    strict = jnp.where(r < c, 1.0, 0.0).astype(BF16)
    before = _dot(cnt.astype(BF16), strict) + carry_ref[:, 0:1]
    rank1 = jnp.sum(jnp.where(oh1, before, 0.0), axis=0, keepdims=True)
    rank2 = jnp.sum(jnp.where(oh2, before, 0.0), axis=0, keepdims=True)
    total = carry_ref[...] + jnp.sum(cnt, axis=1, keepdims=True)
    carry_ref[...] = total
    cnt_ref[...] = total

    o_ref[0:1, :] = e1.astype(F32)
    o_ref[1:2, :] = e2.astype(F32)
    o_ref[2:3, :] = rank1
    o_ref[3:4, :] = rank2
    o_ref[4:5, :] = w1
    o_ref[5:6, :] = w2
    o_ref[6:8, :] = jnp.zeros((2, tt), F32)


def _router(logits, bias_col):
    t = logits.shape[0]
    tt = ROUTER_TT
    return pl.pallas_call(
        _router_kernel,
        grid=(t // tt,),
        in_specs=[
            pl.BlockSpec((tt, LANES), lambda i: (i, 0)),
            pl.BlockSpec((LANES, LANES), lambda i: (0, 0)),
        ],
        out_specs=[
            pl.BlockSpec((SUBLANES, tt), lambda i: (0, i)),
            pl.BlockSpec((N_EXPERTS, LANES), lambda i: (0, 0)),
        ],
        out_shape=[jax.ShapeDtypeStruct((SUBLANES, t), F32), jax.ShapeDtypeStruct((N_EXPERTS, LANES), F32)],
        scratch_shapes=[pltpu.VMEM((N_EXPERTS, LANES), F32)],
        compiler_params=_params("arbitrary"),
        name="router",
    )(logits, bias_col)


TAB_EXPERT, TAB_VALID, TAB_FIRST, TAB_ORD, TAB_SLOT, TAB_NEXT, TAB_DONE, TAB_ROWS = range(8)


def _plan_kernel(r_ref, tot_ref, dest_ref, tab_ref):
    ne = N_EXPERTS
    blk = float(MOE_BLOCK)
    counts = tot_ref[...]
    nblk = jnp.floor((counts + (blk - 1.0)) * (1.0 / blk))
    padded = nblk * blk
    er = lax.broadcasted_iota(I32, (ne, ne), 0)
    ec = lax.broadcasted_iota(I32, (ne, ne), 1)
    tri = jnp.where(ec <= er, 1.0, 0.0)
    pad_ends = _dot_f32(tri, padded)
    pad_starts = pad_ends - padded
    ps_col = pad_starts[:, 0:1]

    tt = r_ref.shape[1]
    eid = lax.broadcasted_iota(I32, (ne, tt), 0).astype(F32)
    for k in range(2):
        start = jnp.sum(jnp.where(eid == r_ref[k:k + 1, :], ps_col, 0.0), axis=0, keepdims=True)
        dest_ref[k:k + 1, :] = (start + r_ref[k + 2:k + 3, :]).astype(I32)

    @pl.when(pl.program_id(0) == 0)
    def _():
        n_phases = float(W_CHUNKS // W_PHASE_CHUNKS)
        nonempty = jnp.where(nblk > 0.0, 1.0, 0.0)
        rank = _dot_f32(tri, nonempty) - 1.0
        slot_e = rank - 2.0 * jnp.floor(rank * 0.5)
        lane0 = jnp.where(lax.broadcasted_iota(I32, (SUBLANES, LANES), 1) == 0, 1.0, 0.0)
        ne_row = _dot_nt(lane0, nonempty)[0:1, :]
        nb_row = _dot_nt(lane0, nblk)[0:1, :]
        ecf = ec.astype(F32)
        cand = jnp.where(ec > er, jnp.where(ne_row > 0.0, ecf, float(ne)), float(ne))
        nxt_e = jnp.min(cand, axis=1, keepdims=True)
        nxt_e = jnp.where(nxt_e == float(ne), -1.0, nxt_e)
        cand = jnp.where(ec < er, jnp.where(ne_row > 0.0, ecf, -1.0), -1.0)
        prev_e = jnp.max(cand, axis=1, keepdims=True)
        nb_prev = jnp.sum(jnp.where(ecf == prev_e, nb_row, 0.0), axis=1, keepdims=True)
        done_e = jnp.where(prev_e >= 0.0, jnp.minimum(nb_prev, n_phases), 0.0)

        nbl = tab_ref.shape[1]
        brow = lax.broadcasted_iota(I32, (1, nbl), 1).astype(F32) * blk
        be = jnp.sum(jnp.where(brow >= pad_ends[:, 0:1], 1.0, 0.0), axis=0, keepdims=True)
        be = jnp.minimum(be, float(ne - 1))
        valid = jnp.where(brow < pad_ends[ne - 1:ne, 0:1], 1.0, 0.0)
        own = lax.broadcasted_iota(I32, (ne, nbl), 0).astype(F32) == be

        def pick(col):
            return jnp.sum(jnp.where(own, col, 0.0), axis=0, keepdims=True)

        ordinal = (brow - pick(ps_col)) * (1.0 / blk)
        first = valid * jnp.where(ordinal == 0.0, 1.0, 0.0)
        used = jnp.clip(pick(counts[:, 0:1]) - ordinal * blk, 0.0, blk)
        rows = {TAB_EXPERT: be, TAB_VALID: valid, TAB_FIRST: first, TAB_ORD: ordinal, TAB_SLOT: pick(slot_e[:, 0:1]),
                TAB_NEXT: pick(nxt_e), TAB_DONE: pick(done_e), TAB_ROWS: used}
        assert len(rows) == tab_ref.shape[0]
        for i, row in rows.items():
            tab_ref[i:i + 1, :] = row.astype(I32)


def _plan(routed, totals, n_blocks):
    t = routed.shape[1]
    tt = ROUTER_TT
    nbl = -(-n_blocks // LANES) * LANES
    return pl.pallas_call(
        _plan_kernel,
        grid=(t // tt,),
        in_specs=[
            pl.BlockSpec((SUBLANES, tt), lambda i: (0, i)),
            pl.BlockSpec((N_EXPERTS, LANES), lambda i: (0, 0)),
        ],
        out_specs=[
            pl.BlockSpec((2, tt), lambda i: (0, i)),
            pl.BlockSpec((SUBLANES, nbl), lambda i: (0, 0)),
        ],
        out_shape=[jax.ShapeDtypeStruct((2, t), I32), jax.ShapeDtypeStruct((SUBLANES, nbl), I32)],
        compiler_params=_params("arbitrary"),
        name="moe_plan",
    )(routed, totals)


def _row_copy(src_ref, src_row, dst_ref, dst_row, sem):
    return pltpu.make_async_copy(src_ref.at[pl.ds(src_row, 1), :], dst_ref.at[pl.ds(dst_row, 1), :], sem)


def _dispatch_kernel(d1_ref, d2_ref, tab_ref, h_ref, xs_ref, zero_ref, sem, zsem):
    tm = h_ref.shape[0]
    base = pl.program_id(0) * tm

    @pl.when(pl.program_id(0) == 0)
    def _():
        zero_ref[...] = jnp.zeros_like(zero_ref)

        def per_block(act):
            def body(b, carry):
                block_rows = xs_ref.at[pl.ds(pl.multiple_of(b * MOE_BLOCK, MOE_BLOCK), MOE_BLOCK), :]

                @pl.when(tab_ref[TAB_VALID, b] == 0)
                def _():
                    act(pltpu.make_async_copy(zero_ref, block_rows, zsem))

                @pl.when(tab_ref[TAB_VALID, b] != 0)
                def _():
                    def one(r, c):
                        act(_row_copy(zero_ref, 0, xs_ref, b * MOE_BLOCK + r, zsem))
                        return c

                    lax.fori_loop(tab_ref[TAB_ROWS, b], MOE_BLOCK, one, 0)

                return carry

            lax.fori_loop(0, xs_ref.shape[0] // MOE_BLOCK, body, 0)

        per_block(lambda cp: cp.start())
        per_block(lambda cp: cp.wait())

    def issue(g, carry):
        for k in range(ROW_UNROLL):
            r = g * ROW_UNROLL + k
            _row_copy(h_ref, r, xs_ref, d1_ref[base + r], sem).start()
            _row_copy(h_ref, r, xs_ref, d2_ref[base + r], sem).start()
        return carry

    lax.fori_loop(0, tm // ROW_UNROLL, issue, 0)

    def drain(g, carry):
        for k in range(ROW_UNROLL):
            r = g * ROW_UNROLL + k
            _row_copy(h_ref, r, xs_ref, 0, sem).wait()
            _row_copy(h_ref, r, xs_ref, 0, sem).wait()
        return carry

    lax.fori_loop(0, tm // ROW_UNROLL, drain, 0)


def _dispatch(dest1, dest2, tables, hn, p_rows):
    t, d = hn.shape
    tm = ROW_TM
    grid_spec = pltpu.PrefetchScalarGridSpec(
        num_scalar_prefetch=3,
        grid=(t // tm,),
        in_specs=[pl.BlockSpec((tm, d), lambda i, *_: (i, 0))],
        out_specs=pl.BlockSpec(memory_space=pl.ANY),
        scratch_shapes=[pltpu.VMEM((MOE_BLOCK, d), hn.dtype), pltpu.SemaphoreType.DMA, pltpu.SemaphoreType.DMA],
    )
    return pl.pallas_call(
        _dispatch_kernel,
        grid_spec=grid_spec,
        out_shape=jax.ShapeDtypeStruct((p_rows, d), hn.dtype),
        compiler_params=_params("arbitrary"),
        name="moe_dispatch",
    )(dest1, dest2, tables, hn)


def _weight_chunk(c, d, de):
    per = W_CHUNKS // 3
    if c < 2 * per:
        return c // per, (c % per) * (d // per), 0
    r, h = divmod(c - 2 * per, d // de)
    return 2, r * (d // per), h * de


def _moe_kernel(tab_ref, x_ref, wg_hbm, wu_hbm, wd_hbm, y_ref, wg_c, wu_c, wd_c, stage_ref, sems, *, layer):
    b = pl.program_id(0)
    expert = tab_ref[TAB_EXPERT, b]
    valid = tab_ref[TAB_VALID, b]
    first = tab_ref[TAB_FIRST, b]
    ordinal = tab_ref[TAB_ORD, b]
    slot = tab_ref[TAB_SLOT, b]
    nxt = tab_ref[TAB_NEXT, b]
    done = tab_ref[TAB_DONE, b]
    hbm = (wg_hbm, wu_hbm, wd_hbm)
    cache = (wg_c, wu_c, wd_c)
    d, de = wg_c.shape[1], wg_c.shape[2]
    rows, cols = stage_ref.shape[1], stage_ref.shape[2]

    sub = rows // W_CHUNK_SPLIT

    def chunk_copies(e, c, k):
        ti, r0, c0 = _weight_chunk(c, d, de)
        return [pltpu.make_async_copy(hbm[ti].at[layer, e, pl.ds(r0 + i * sub, sub), pl.ds(c0, cols)],
                                      stage_ref.at[k, pl.ds(i * sub, sub), :], sems.at[k])
                for i in range(W_CHUNK_SPLIT)]

    def start_phase(e, p):
        for k in range(W_PHASE_CHUNKS):
            for cp in chunk_copies(e, p * W_PHASE_CHUNKS + k, k):
                cp.start()

    def finish_phase(e, s, p):
        for k in range(W_PHASE_CHUNKS):
            c = p * W_PHASE_CHUNKS + k
            for cp in chunk_copies(e, c, k):
                cp.wait()
            ti, r0, c0 = _weight_chunk(c, d, de)
            cache[ti][s, pl.ds(r0, rows), pl.ds(c0, cols)] = stage_ref[k].astype(BF16)

    n_phases = W_CHUNKS // W_PHASE_CHUNKS

    @pl.when(valid != 0)
    def _():
        @pl.when(first != 0)
        def _():
            for p in range(n_phases):
                @pl.when(done <= p)
                def _():
                    start_phase(expert, p)
                    finish_phase(expert, slot, p)

        prefetch = (ordinal < n_phases) & (nxt >= 0)
        for p in range(n_phases):
            @pl.when(prefetch & (ordinal == p))
            def _():
                start_phase(nxt, p)

        x = _unpack_halves(x_ref[...])
        g = _dot(x, wg_c[slot])
        u = _dot(x, wu_c[slot])
        hid = (g * jax.nn.sigmoid(g) * u).astype(BF16)
        y_ref[...] = _pack_halves(_dot(hid, wd_c[slot]))

        for p in range(n_phases):
            @pl.when(prefetch & (ordinal == p))
            def _():
                finish_phase(nxt, 1 - slot, p)

    @pl.when(valid == 0)
    def _():
        y_ref[...] = jnp.zeros_like(y_ref)


def _moe(tables, xs, wg, wu, wd, layer):
    p_rows, dh = xs.shape
    tm = MOE_BLOCK
    d, de = wg.shape[2], wg.shape[3]
    assert d == 2 * dh and wd.shape[2:] == (de, d) and d % (W_CHUNKS // 3) == 0 and d % de == 0
    chunk = (d // (W_CHUNKS // 3), de)
    grid_spec = pltpu.PrefetchScalarGridSpec(
        num_scalar_prefetch=1,
        grid=(p_rows // tm,),
        in_specs=[
            pl.BlockSpec((tm, dh), lambda i, *_: (i, 0)),
            pl.BlockSpec(memory_space=pl.ANY),
            pl.BlockSpec(memory_space=pl.ANY),
            pl.BlockSpec(memory_space=pl.ANY),
        ],
        out_specs=pl.BlockSpec((tm, dh), lambda i, *_: (i, 0)),
        scratch_shapes=[
            pltpu.VMEM((2, d, de), BF16), pltpu.VMEM((2, d, de), BF16), pltpu.VMEM((2, de, d), BF16),
            pltpu.VMEM((W_PHASE_CHUNKS,) + chunk, F32), pltpu.SemaphoreType.DMA((W_PHASE_CHUNKS,)),
        ],
    )
    return pl.pallas_call(
        functools.partial(_moe_kernel, layer=layer),
        grid_spec=grid_spec,
        out_shape=jax.ShapeDtypeStruct((p_rows, dh), I32),
        compiler_params=_params("arbitrary"),
        name="moe_experts",
    )(tables, xs, wg, wu, wd)


def _combine_kernel(d1_ref, d2_ref, x_ref, w_ref, ys_ref, o_ref, buf1, buf2, sem):
    tm = x_ref.shape[0]
    base = pl.program_id(0) * tm

    def issue(g, carry):
        for k in range(ROW_UNROLL):
            r = g * ROW_UNROLL + k
            _row_copy(ys_ref, d1_ref[base + r], buf1, r, sem).start()
            _row_copy(ys_ref, d2_ref[base + r], buf2, r, sem).start()
        return carry

    lax.fori_loop(0, tm // ROW_UNROLL, issue, 0)

    def drain(g, carry):
        for k in range(ROW_UNROLL):
            r = g * ROW_UNROLL + k
            _row_copy(ys_ref, 0, buf1, r, sem).wait()
            _row_copy(ys_ref, 0, buf2, r, sem).wait()
        return carry

    lax.fori_loop(0, tm // ROW_UNROLL, drain, 0)
    w = w_ref[...]
    y1 = _unpack_halves(buf1[...]).astype(F32)
    y2 = _unpack_halves(buf2[...]).astype(F32)
    o_ref[...] = x_ref[...] + w[:, 0:1] * y1 + w[:, 1:2] * y2


def _combine(dest1, dest2, x, w12, ys):
    t, d = x.shape
    tm = ROW_TM
    dh = ys.shape[1]
    grid_spec = pltpu.PrefetchScalarGridSpec(
        num_scalar_prefetch=2,
        grid=(t // tm,),
        in_specs=[
            pl.BlockSpec((tm, d), lambda i, a, b: (i, 0)),
            pl.BlockSpec((tm, 2), lambda i, a, b: (i, 0)),
            pl.BlockSpec(memory_space=pl.ANY),
        ],
        out_specs=pl.BlockSpec((tm, d), lambda i, a, b: (i, 0)),
        scratch_shapes=[pltpu.VMEM((tm, dh), I32), pltpu.VMEM((tm, dh), I32), pltpu.SemaphoreType.DMA],
    )
    return pl.pallas_call(
        _combine_kernel,
        grid_spec=grid_spec,
        out_shape=jax.ShapeDtypeStruct((t, d), F32),
        compiler_params=_params("arbitrary"),
        name="moe_combine",
    )(dest1, dest2, x, w12, ys)


def _pad_cols(w, width):
    return jnp.pad(w, ((0, 0), (0, width - w.shape[1])))


def _moe_layer(x_mid, hn, logits, b_grp, b_exp, w_gate, w_up, w_down, layer):
    t, d = x_mid.shape
    bias = jnp.zeros((LANES,), F32)
    bias = bias.at[ROUTER_GROUP_ROW:ROUTER_GROUP_ROW + N_GROUPS].set(b_grp)
    bias = bias.at[ROUTER_EXPERT_ROW:ROUTER_EXPERT_ROW + N_EXPERTS].set(b_exp)
    routed, totals = _router(logits, jnp.broadcast_to(bias[:, None], (LANES, LANES)))
    p_rows = 2 * t + N_EXPERTS * MOE_BLOCK
    dest, tables = _plan(routed, totals, p_rows // MOE_BLOCK)
    dest1, dest2 = dest[0], dest[1]
    xs = _dispatch(dest1, dest2, tables, hn, p_rows)
    ys = _moe(tables, xs, w_gate, w_up, w_down, layer)
    w12 = jnp.stack([routed[4], routed[5]], axis=-1)
    return _combine(dest1, dest2, x_mid, w12, ys)


def _router_weight(w_grp, w_exp):
    d = w_grp.shape[0]
    wr = jnp.zeros((d, LANES), F32)
    wr = wr.at[:, ROUTER_GROUP_ROW:ROUTER_GROUP_ROW + N_GROUPS].set(w_grp)
    wr = wr.at[:, ROUTER_EXPERT_ROW:ROUTER_EXPERT_ROW + N_EXPERTS].set(w_exp)
    hi = wr.astype(BF16)
    lo = (wr - hi.astype(F32)).astype(BF16)
    return jnp.concatenate([hi, lo], axis=1)


def _fox_layer(x2, attn_g, w_in, b_f, q_g, k_g, mem_kv, memq_g, memk_g, batch, seq, mem_len):
    qkvo = 4 * FOX_W
    w_main = jnp.concatenate([w_in[:, :qkvo], w_in[:, qkvo + FOX_HEADS:]], axis=1).astype(BF16)
    w_gate_cols = _pad_cols(w_in[:, qkvo:qkvo + FOX_HEADS], GATE_GROUP)
    w_side = _pad_cols(jnp.concatenate([w_gate_cols] * 3, axis=1), LANES).astype(BF16)
    scale = LOG2E / math.sqrt(HEAD_DIM)
    head_gain = jnp.concatenate([jnp.tile(q_g * scale, FOX_HEADS), jnp.tile(k_g, FOX_HEADS),
                                 jnp.ones((w_main.shape[1] - 2 * FOX_W,), F32)]).reshape(1, -1)
    proj, f_side = _norm_matmul(x2, attn_g, w_main, head_gain, w_side, n_norm_tiles=2 * FOX_W // PROJ_TN,
                                tm=PROJ_TM, name="fox_in_proj")
    kb = _fox_decay(f_side, b_f, batch, seq)
    mix = _fox_attention(proj, kb, batch, seq)
    mo = _memory_attention(proj, qkvo // MEM_W, mem_kv, memq_g, memk_g, batch, seq, mem_len)
    return mix, mo


def _gla_layer(x2, attn_g, w_in, w_a2, b_a, o_g, mem_kv, memq_g, memk_g, batch, seq, mem_len):
    d = w_in.shape[0]
    kw = GLA_HEADS * GLA_DK
    vw = GLA_HEADS * GLA_DV
    pad = GLA_DKP - GLA_DK

    def pad_heads(w):
        lead = w.shape[0]
        return jnp.pad(w.reshape(lead, GLA_HEADS, GLA_DK), ((0, 0), (0, 0), (0, pad))).reshape(lead, -1)

    wq = pad_heads(w_in[:, :kw] * (1.0 / math.sqrt(GLA_DK)))
    wk = pad_heads(w_in[:, kw:2 * kw])
    rest = w_in[:, 2 * kw:2 * kw + 2 * vw]
    w_a1 = w_in[:, 2 * kw + 2 * vw:2 * kw + 2 * vw + GLA_RANK]
    w_qm = w_in[:, 2 * kw + 2 * vw + GLA_RANK:]
    w_main = jnp.concatenate([rest, wq, wk, w_qm], axis=1).astype(BF16)
    w_side = _pad_cols(w_a1, LANES).astype(BF16)
    head_gain = jnp.ones((1, w_main.shape[1]), F32)
    proj, a_side = _norm_matmul(x2, attn_g, w_main, head_gain, w_side, n_norm_tiles=0, tm=PROJ_TM,
                                name="gla_in_proj")
    wa = jnp.pad(pad_heads(w_a2), ((0, LANES - GLA_RANK), (0, 0)))
    ba = pad_heads(b_a.reshape(1, kw))
    mix = _gla(proj, a_side, wa, ba, o_g, batch, seq)
    q_col_block = (2 * GLA_HEADS * GLA_DKP + 2 * vw) // MEM_W
    mo = _memory_attention(proj, q_col_block, mem_kv, memq_g, memk_g, batch, seq, mem_len)
    return mix, mo


def kernel(x, mem, mem_norm_g, w_mem_kv, attn_norm_g, fox_w_in, fox_b_f, fox_q_g, fox_k_g, gla_w_in, gla_w_a2,
           gla_b_a, gla_o_g, memq_g, memk_g, w_out, ffn_norm_g, w_grp, b_grp, w_exp, b_exp, w_gate, w_up, w_down):
    batch, seq, d = x.shape
    mem_len = mem.shape[1]
    depth = attn_norm_g.shape[0]
    x2 = x.reshape(batch * seq, d)

    n_kv = w_mem_kv.shape[1]
    mem_kv, _ = _norm_matmul(mem.reshape(batch * mem_len, d), mem_norm_g, w_mem_kv.astype(BF16),
                             jnp.ones((1, n_kv), F32), jnp.zeros((d, LANES), BF16), n_norm_tiles=0,
                             tm=batch * mem_len, name="mem_kv_proj")

    for i in range(depth):
        j = i // 2
        if i % 2 == 0:
            mix, mo = _fox_layer(x2, attn_norm_g[i], fox_w_in[j], fox_b_f[j], fox_q_g[j], fox_k_g[j],
                                 mem_kv, memq_g[i], memk_g[i], batch, seq, mem_len)
        else:
            mix, mo = _gla_layer(x2, attn_norm_g[i], gla_w_in[j], gla_w_a2[j], gla_b_a[j], gla_o_g[j],
                                 mem_kv, memq_g[i], memk_g[i], batch, seq, mem_len)
        w_o = w_out[i].astype(BF16)
        x_mid, hn, logits = _out_proj(mix, mo, w_o[:mix.shape[1]], w_o[mix.shape[1]:], x2, ffn_norm_g[i],
                                      _router_weight(w_grp[i], w_exp[i]))
        x2 = _moe_layer(x_mid, hn, logits, b_grp[i], b_exp[i], w_gate, w_up, w_down, i)
    return x2.reshape(batch, seq, d)
```

```python
import functools
import math

import jax
import jax.numpy as jnp
from jax import lax
from jax.experimental import pallas as pl
from jax.experimental.pallas import tpu as pltpu

F32 = jnp.float32
BF16 = jnp.bfloat16
I32 = jnp.int32

HEAD_DIM = 128
FOX_HEADS = 12
FOX_W = FOX_HEADS * HEAD_DIM
MEM_HEADS = 4
MEM_W = MEM_HEADS * HEAD_DIM
GLA_HEADS = 4
GLA_DV = FOX_W // GLA_HEADS
GLA_DK = GLA_DV // 2
GLA_DKP = 256
GLA_RANK = 16
GLA_TAU = 16.0
N_GROUPS = 4
EXPERTS_PER_GROUP = 8
N_EXPERTS = N_GROUPS * EXPERTS_PER_GROUP
RMS_EPS = 1e-6
LOG2E = 1.4426950408889634
GATE_GROUP = 16

LANES = 128
SUBLANES = 8
VMEM_LIMIT_BYTES = 56 * 1024 * 1024

PROJ_TM = 1024
PROJ_TN = 512
ATTN_T = 512
ATTN_HEADS_PER_STEP = 6
DECAY_TS = 512
MEM_TQ = 512
OUT_TM = 512
ROUTER_TT = 512
ROW_TM = 256
MOE_BLOCK = 256
ROW_UNROLL = 8
W_CHUNKS = 12
W_PHASE_CHUNKS = 3
W_CHUNK_SPLIT = 1
GLA_TS = 256
GLA_CHUNK = 64
ROUTER_GROUP_ROW = 0
ROUTER_EXPERT_ROW = 8
NEG_BIG = -1e30


def _params(*sem):
    return pltpu.CompilerParams(dimension_semantics=sem, vmem_limit_bytes=VMEM_LIMIT_BYTES)


def _log_sigmoid(z):
    return jnp.minimum(z, 0.0) - jnp.log1p(jnp.exp(-jnp.abs(z)))


def _dot(a, b):
    return jnp.dot(a, b, preferred_element_type=F32)


def _dot_nt(a, b):
    return lax.dot_general(a, b, (((1,), (1,)), ((), ())), preferred_element_type=F32)


def _dot_f32(a, b):
    return jnp.dot(a, b, preferred_element_type=F32, precision=lax.Precision.HIGHEST)


def _pack_halves(x):
    n = x.shape[1] // 2
    lo = lax.bitcast_convert_type(x[:, :n].astype(BF16).astype(F32), I32)
    hi = lax.bitcast_convert_type(x[:, n:].astype(BF16).astype(F32), I32)
    return lax.shift_right_logical(lo, 16) | hi


def _unpack_halves(w):
    lo = lax.bitcast_convert_type(lax.shift_left(w, 16), F32)
    hi = lax.bitcast_convert_type(w & jnp.int32(-65536), F32)
    return jnp.concatenate([lo.astype(BF16), hi.astype(BF16)], axis=1)


def _split3(x):
    hi = x.astype(BF16)
    r1 = x - hi.astype(F32)
    mid = r1.astype(BF16)
    lo = (r1 - mid.astype(F32)).astype(BF16)
    return hi, mid, lo


def _norm_matmul_kernel(x_ref, g_ref, w_ref, hg_ref, ws_ref, o_ref, os_ref, xn_ref, *, nj_main, n_norm_tiles):
    j = pl.program_id(1)

    @pl.when(j == 0)
    def _():
        x = x_ref[...]
        ms = jnp.mean(x * x, axis=-1, keepdims=True)
        xn_ref[...] = (x * lax.rsqrt(ms + RMS_EPS) * g_ref[...]).astype(BF16)

    if n_norm_tiles > 0:
        @pl.when(j < n_norm_tiles)
        def _():
            acc = _dot(xn_ref[...], w_ref[...])
            for h in range(acc.shape[1] // HEAD_DIM):
                sl = slice(h * HEAD_DIM, (h + 1) * HEAD_DIM)
                a = acc[:, sl]
                ms = jnp.mean(a * a, axis=-1, keepdims=True)
                o_ref[:, sl] = (a * lax.rsqrt(ms + RMS_EPS) * hg_ref[:, sl]).astype(o_ref.dtype)

    @pl.when((j >= n_norm_tiles) & (j < nj_main))
    def _():
        o_ref[...] = _dot(xn_ref[...], w_ref[...]).astype(o_ref.dtype)

    @pl.when(j == nj_main)
    def _():
        os_ref[...] = _dot(xn_ref[...], ws_ref[...])


def _norm_matmul(x, g, w_main, head_gain, w_side, *, n_norm_tiles, tm, name):
    t, d = x.shape
    n_main = w_main.shape[1]
    tn = PROJ_TN
    nj_main = n_main // tn
    n_side = w_side.shape[1]
    assert t % tm == 0 and n_main % tn == 0 and n_side % LANES == 0
    last = nj_main - 1
    kernel = functools.partial(_norm_matmul_kernel, nj_main=nj_main, n_norm_tiles=n_norm_tiles)
    return pl.pallas_call(
        kernel,
        grid=(t // tm, nj_main + 1),
        in_specs=[
            pl.BlockSpec((tm, d), lambda i, j: (i, 0)),
            pl.BlockSpec((1, d), lambda i, j: (0, 0)),
            pl.BlockSpec((d, tn), lambda i, j: (0, jnp.minimum(j, last))),
            pl.BlockSpec((1, tn), lambda i, j: (0, jnp.minimum(j, last))),
            pl.BlockSpec((d, n_side), lambda i, j: (0, 0)),
        ],
        out_specs=[
            pl.BlockSpec((tm, tn), lambda i, j: (i, jnp.minimum(j, last))),
            pl.BlockSpec((tm, n_side), lambda i, j: (i, 0)),
        ],
        out_shape=[jax.ShapeDtypeStruct((t, n_main), BF16), jax.ShapeDtypeStruct((t, n_side), F32)],
        scratch_shapes=[pltpu.VMEM((tm, d), BF16)],
        compiler_params=_params("parallel", "arbitrary"),
        name=name,
    )(x, g.reshape(1, d), w_main, head_gain, w_side)


def _fox_decay_kernel(f_ref, b_ref, kb_ref, carry_ref):
    @pl.when(pl.program_id(1) == 0)
    def _():
        carry_ref[...] = jnp.zeros_like(carry_ref)

    lane = lax.broadcasted_iota(I32, f_ref.shape, 1)
    f = jnp.where(lane < FOX_HEADS, f_ref[...], 0.0)
    f = f + pltpu.roll(f, GATE_GROUP, 1) + pltpu.roll(f, 2 * GATE_GROUP, 1)
    lf = _log_sigmoid(f + b_ref[...]) * (-LOG2E)
    ts = lf.shape[0]
    r = lax.broadcasted_iota(I32, (ts, ts), 0)
    c = lax.broadcasted_iota(I32, (ts, ts), 1)
    incl = jnp.where(c <= r, 1.0, 0.0).astype(BF16)
    hi, mid, lo = _split3(lf)
    cs = _dot(incl, hi) + _dot(incl, mid) + _dot(incl, lo) + carry_ref[0:1, :]
    carry_ref[...] = jnp.broadcast_to(cs[ts - 1:ts, :], carry_ref.shape)
    hi, mid, lo = _split3(cs)
    piece = jnp.where(lane < GATE_GROUP, hi.astype(F32),
                      jnp.where(lane < 2 * GATE_GROUP, mid.astype(F32),
                                jnp.where(lane < 3 * GATE_GROUP, lo.astype(F32), 0.0)))
    kb_ref[...] = piece.astype(BF16)


def _fox_decay(f_side, b_f, batch, seq):
    ts = DECAY_TS
    ns = seq // ts
    b_pad = jnp.zeros((LANES,), F32)
    for p in range(3):
        b_pad = b_pad.at[p * GATE_GROUP:p * GATE_GROUP + FOX_HEADS].set(b_f)
    return pl.pallas_call(
        _fox_decay_kernel,
        grid=(batch, ns),
        in_specs=[
            pl.BlockSpec((ts, LANES), lambda b, s: (b * ns + s, 0)),
            pl.BlockSpec((1, LANES), lambda b, s: (0, 0)),
        ],
        out_specs=pl.BlockSpec((ts, LANES), lambda b, s: (b * ns + s, 0)),
        out_shape=jax.ShapeDtypeStruct((batch * seq, LANES), BF16),
        scratch_shapes=[pltpu.VMEM((SUBLANES, LANES), F32)],
        compiler_params=_params("parallel", "arbitrary"),
        name="fox_decay",
    )(f_side, b_pad.reshape(1, LANES))


def _fox_attn_kernel(q_ref, k_ref, v_ref, kb_ref, og_ref, o_ref, m_ref, acc_ref):
    hp = pl.program_id(1)
    qi = pl.program_id(2)
    t = q_ref.shape[0]
    lane = lax.broadcasted_iota(I32, (t, LANES), 1)
    ones = jnp.ones((t, HEAD_DIM), BF16)
    q_aug = []
    for hh in range(ATTN_HEADS_PER_STEP):
        h = hp * ATTN_HEADS_PER_STEP + hh
        own = jnp.where(lane < 3 * GATE_GROUP, jnp.where((lane & (GATE_GROUP - 1)) == h, 1.0, 0.0), 0.0)
        q_aug.append(jnp.concatenate([q_ref[:, hh * HEAD_DIM:(hh + 1) * HEAD_DIM], own.astype(BF16)], axis=1))
    m_ref[...] = jnp.full_like(m_ref, NEG_BIG)
    acc_ref[...] = jnp.zeros_like(acc_ref)

    def tile(j, masked):
        rows = pl.ds(pl.multiple_of(j * t, t), t)
        kb = kb_ref[rows, :]
        for hh in range(ATTN_HEADS_PER_STEP):
            cols = slice(hh * HEAD_DIM, (hh + 1) * HEAD_DIM)
            k_aug = jnp.concatenate([k_ref[rows, cols], kb], axis=1)
            s = _dot_nt(q_aug[hh], k_aug)
            if masked:
                row = lax.broadcasted_iota(I32, (t, t), 0)
                col = lax.broadcasted_iota(I32, (t, t), 1)
                s = jnp.where(col <= row, s, NEG_BIG)
            m_prev = m_ref[hh]
            m_new = jnp.maximum(m_prev, jnp.max(s, axis=-1, keepdims=True))
            alpha = jnp.exp2(m_prev - m_new)
            p = jnp.exp2(s - jnp.concatenate([m_new] * (t // LANES), axis=1))
            v_aug = jnp.concatenate([v_ref[rows, cols], ones], axis=1)
            pv = _dot(p.astype(BF16), v_aug)
            acc_ref[hh] = jnp.concatenate([alpha, alpha], axis=1) * acc_ref[hh] + pv
            m_ref[hh] = m_new

    def full_tile(j, carry):
        tile(j, False)
        return carry

    lax.fori_loop(0, qi, full_tile, 0)
    tile(qi, True)
    for hh in range(ATTN_HEADS_PER_STEP):
        cols = slice(hh * HEAD_DIM, (hh + 1) * HEAD_DIM)
        acc = acc_ref[hh]
        og = og_ref[:, cols].astype(F32)
        o_ref[:, cols] = (acc[:, :HEAD_DIM] / acc[:, HEAD_DIM:] * jax.nn.sigmoid(og)).astype(o_ref.dtype)


def _fox_attention(proj, kb, batch, seq):
    t = ATTN_T
    nq = seq // t
    hps = ATTN_HEADS_PER_STEP
    w = hps * HEAD_DIM
    nhp = FOX_HEADS // hps
    return pl.pallas_call(
        _fox_attn_kernel,
        grid=(batch, nhp, nq),
        in_specs=[
            pl.BlockSpec((t, w), lambda b, h, i: (b * nq + i, h)),
            pl.BlockSpec((seq, w), lambda b, h, i: (b, nhp + h)),
            pl.BlockSpec((seq, w), lambda b, h, i: (b, 2 * nhp + h)),
            pl.BlockSpec((seq, LANES), lambda b, h, i: (b, 0)),
            pl.BlockSpec((t, w), lambda b, h, i: (b * nq + i, 3 * nhp + h)),
        ],
        out_specs=pl.BlockSpec((t, w), lambda b, h, i: (b * nq + i, h)),
        out_shape=jax.ShapeDtypeStruct((batch * seq, FOX_W), BF16),
        scratch_shapes=[pltpu.VMEM((hps, t, LANES), F32), pltpu.VMEM((hps, t, 2 * HEAD_DIM), F32)],
        compiler_params=_params("parallel", "parallel", "arbitrary"),
        name="fox_attention",
    )(proj, proj, proj, kb, proj)


def _mem_attn_kernel(q_ref, k_ref, v_ref, qg_ref, kg_ref, o_ref, *, q_offset):
    for h in range(MEM_HEADS):
        sl = slice(h * HEAD_DIM, (h + 1) * HEAD_DIM)
        q = q_ref[:, q_offset + h * HEAD_DIM:q_offset + (h + 1) * HEAD_DIM].astype(F32)
        q = q * lax.rsqrt(jnp.mean(q * q, axis=-1, keepdims=True) + RMS_EPS) * qg_ref[...]
        k = k_ref[:, sl].astype(F32)
        k = k * lax.rsqrt(jnp.mean(k * k, axis=-1, keepdims=True) + RMS_EPS) * kg_ref[...]
        s = _dot_nt(q.astype(BF16), k.astype(BF16))
        p = jnp.exp(s - jnp.max(s, axis=-1, keepdims=True))
        o = _dot(p.astype(BF16), v_ref[:, sl]) / jnp.sum(p, axis=-1, keepdims=True)
        o_ref[:, sl] = o.astype(o_ref.dtype)


def _memory_attention(q_src, q_block_w, q_col_block, q_offset, mem_kv, q_gain, k_gain, batch, seq, mem_len):
    tq = MEM_TQ
    nq = seq // tq
    scale = 1.0 / math.sqrt(HEAD_DIM)
    return pl.pallas_call(
        functools.partial(_mem_attn_kernel, q_offset=q_offset),
        grid=(batch * nq,),
        in_specs=[
            pl.BlockSpec((tq, q_block_w), lambda i: (i, q_col_block)),
            pl.BlockSpec((mem_len, MEM_W), lambda i: (i // nq, 0)),
            pl.BlockSpec((mem_len, MEM_W), lambda i: (i // nq, 1)),
            pl.BlockSpec((1, HEAD_DIM), lambda i: (0, 0)),
            pl.BlockSpec((1, HEAD_DIM), lambda i: (0, 0)),
        ],
        out_specs=pl.BlockSpec((tq, MEM_W), lambda i: (i, 0)),
        out_shape=jax.ShapeDtypeStruct((batch * seq, MEM_W), BF16),
        compiler_params=_params("parallel"),
        name="memory_attention",
    )(q_src, mem_kv, mem_kv, (q_gain * scale).reshape(1, HEAD_DIM), k_gain.reshape(1, HEAD_DIM))


def _gla_kernel(q_ref, k_ref, v_ref, r_ref, a_ref, wa_ref, ba_ref, og_ref, o_ref,
                st_ref, qe_ref, q2_ref, k2_ref, kd_ref, oi_ref):
    @pl.when(pl.program_id(1) == 0)
    def _():
        st_ref[...] = jnp.zeros_like(st_ref)

    ts = q_ref.shape[0]
    c = GLA_CHUNK
    shift = c.bit_length() - 1
    n_chunks = ts // c

    z = _dot_f32(a_ref[...], wa_ref[...]) + ba_ref[...]
    g = _log_sigmoid(z) * (1.0 / GLA_TAU)
    ri = lax.broadcasted_iota(I32, (ts, ts), 0)
    ci = lax.broadcasted_iota(I32, (ts, ts), 1)
    chunk_start = lax.shift_left(lax.shift_right_logical(ri, shift), shift)

    def chunk_causal(val):
        return jnp.where(ci <= ri, jnp.where(ci >= chunk_start, val, 0.0), 0.0)

    incl = chunk_causal(jnp.ones((ts, ts), F32)).astype(BF16)
    hi, mid, lo = _split3(g)
    bc = _dot(incl, hi) + _dot(incl, mid) + _dot(incl, lo)
    decay = []
    for ic in range(n_chunks):
        rows = slice(ic * c, (ic + 1) * c)
        bcc = bc[rows]
        b_mid = bcc[c // 2 - 1:c // 2]
        b_last = bcc[c - 1:c]
        q = q_ref[rows, :].astype(F32)
        k = k_ref[rows, :].astype(F32)
        qe_ref[rows, :] = (q * jnp.exp(bcc)).astype(BF16)
        q2_ref[rows, :] = (q * jnp.exp(bcc - b_mid)).astype(BF16)
        k2_ref[rows, :] = (k * jnp.exp(b_mid - bcc)).astype(BF16)
        kd_ref[rows, :] = (k * jnp.exp(b_last - bcc)).astype(BF16)
        decay.append(jnp.exp(b_last))
    for h in range(GLA_HEADS):
        ks = slice(h * GLA_DKP, (h + 1) * GLA_DKP)
        vs = slice(h * GLA_DV, (h + 1) * GLA_DV)
        att = chunk_causal(_dot_nt(q2_ref[:, ks], k2_ref[:, ks]))
        oi_ref[:, vs] = _dot(att.astype(BF16), v_ref[:, vs])

    for ic in range(n_chunks):
        rows = slice(ic * c, (ic + 1) * c)
        for h in range(GLA_HEADS):
            ks = slice(h * GLA_DKP, (h + 1) * GLA_DKP)
            vs = slice(h * GLA_DV, (h + 1) * GLA_DV)
            st = st_ref[h]
            o = _dot_nt(qe_ref[rows, ks], st.astype(BF16)) + oi_ref[rows, vs]
            upd = lax.dot_general(v_ref[rows, vs], kd_ref[rows, ks], (((0,), (0,)), ((), ())),
                                  preferred_element_type=F32)
            st_ref[h] = st * decay[ic][:, ks] + upd
            on = o * lax.rsqrt(jnp.mean(o * o, axis=-1, keepdims=True) + RMS_EPS) * og_ref[...]
            rh = r_ref[rows, vs].astype(F32)
            o_ref[rows, vs] = (on * rh * jax.nn.sigmoid(rh)).astype(o_ref.dtype)


def _gla(proj, a_side, wa, ba, o_gain, batch, seq):
    ts = GLA_TS
    ns = seq // ts
    kw = GLA_HEADS * GLA_DKP
    vw = GLA_HEADS * GLA_DV
    assert (2 * vw) % kw == 0
    q_blk = (2 * vw) // kw
    return pl.pallas_call(
        _gla_kernel,
        grid=(batch, ns),
        in_specs=[
            pl.BlockSpec((ts, kw), lambda b, s: (b * ns + s, q_blk)),
            pl.BlockSpec((ts, kw), lambda b, s: (b * ns + s, q_blk + 1)),
            pl.BlockSpec((ts, vw), lambda b, s: (b * ns + s, 0)),
            pl.BlockSpec((ts, vw), lambda b, s: (b * ns + s, 1)),
            pl.BlockSpec((ts, LANES), lambda b, s: (b * ns + s, 0)),
            pl.BlockSpec((LANES, kw), lambda b, s: (0, 0)),
            pl.BlockSpec((1, kw), lambda b, s: (0, 0)),
            pl.BlockSpec((1, GLA_DV), lambda b, s: (0, 0)),
        ],
        out_specs=pl.BlockSpec((ts, vw), lambda b, s: (b * ns + s, 0)),
        out_shape=jax.ShapeDtypeStruct((batch * seq, vw), BF16),
        scratch_shapes=[pltpu.VMEM((GLA_HEADS, GLA_DV, GLA_DKP), F32)] + [pltpu.VMEM((ts, kw), BF16)] * 4
        + [pltpu.VMEM((ts, vw), F32)],
        compiler_params=_params("parallel", "arbitrary"),
        name="gla",
    )(proj, proj, proj, proj, a_side, wa, ba, o_gain.reshape(1, GLA_DV))


def _out_proj_kernel(mix_ref, mo_ref, w1_ref, w2_ref, x_ref, g_ref, wr_ref, xo_ref, hn_ref, lg_ref):
    y = _dot(mix_ref[...], w1_ref[...]) + _dot(mo_ref[...], w2_ref[...])
    xn = x_ref[...] + y
    xo_ref[...] = xn
    hn = xn * lax.rsqrt(jnp.mean(xn * xn, axis=-1, keepdims=True) + RMS_EPS) * g_ref[...]
    hn_ref[...] = _pack_halves(hn)
    h_hi = hn.astype(BF16)
    h_lo = (hn - h_hi.astype(F32)).astype(BF16)
    both = _dot(h_hi, wr_ref[...])
    lg_ref[...] = both[:, :LANES] + both[:, LANES:] + _dot(h_lo, wr_ref[:, :LANES])


def _out_proj(mix, mo, w1, w2, x, g, wr):
    t, d = x.shape
    tm = OUT_TM
    once = pl.Buffered(1)
    return pl.pallas_call(
        _out_proj_kernel,
        grid=(t // tm,),
        in_specs=[
            pl.BlockSpec((tm, mix.shape[1]), lambda i: (i, 0)),
            pl.BlockSpec((tm, mo.shape[1]), lambda i: (i, 0)),
            pl.BlockSpec(w1.shape, lambda i: (0, 0), pipeline_mode=once),
            pl.BlockSpec(w2.shape, lambda i: (0, 0), pipeline_mode=once),
            pl.BlockSpec((tm, d), lambda i: (i, 0)),
            pl.BlockSpec((1, d), lambda i: (0, 0), pipeline_mode=once),
            pl.BlockSpec((d, 2 * LANES), lambda i: (0, 0), pipeline_mode=once),
        ],
        out_specs=[
            pl.BlockSpec((tm, d), lambda i: (i, 0)),
            pl.BlockSpec((tm, d // 2), lambda i: (i, 0)),
            pl.BlockSpec((tm, LANES), lambda i: (i, 0)),
        ],
        out_shape=[jax.ShapeDtypeStruct((t, d), F32), jax.ShapeDtypeStruct((t, d // 2), I32),
                   jax.ShapeDtypeStruct((t, LANES), F32)],
        compiler_params=_params("parallel"),
        name="out_proj",
    )(mix, mo, w1, w2, x, g.reshape(1, d), wr)


def _router_kernel(lg_ref, b_ref, o_ref, cnt_ref, carry_ref):
    @pl.when(pl.program_id(0) == 0)
    def _():
        carry_ref[...] = jnp.zeros_like(carry_ref)

    logits = lg_ref[...].T + b_ref[:, 0:1]
    tt = logits.shape[1]
    gl = [logits[ROUTER_GROUP_ROW + g:ROUTER_GROUP_ROW + g + 1, :] for g in range(N_GROUPS)]
    best = gl[0]
    gidx = jnp.zeros((1, tt), I32)
    for g in range(1, N_GROUPS):
        better = gl[g] > best
        best = jnp.where(better, gl[g], best)
        gidx = jnp.where(better, g, gidx)
    denom = jnp.exp(gl[0] - best)
    for g in range(1, N_GROUPS):
        denom = denom + jnp.exp(gl[g] - best)
    g_p = 1.0 / denom

    epg = EXPERTS_PER_GROUP
    in_grp = logits[ROUTER_EXPERT_ROW:ROUTER_EXPERT_ROW + epg, :]
    for g in range(1, N_GROUPS):
        lo = ROUTER_EXPERT_ROW + g * epg
        in_grp = jnp.where(gidx == g, logits[lo:lo + epg, :], in_grp)
    rows = lax.broadcasted_iota(I32, (epg, tt), 0).astype(F32)
    m1 = jnp.max(in_grp, axis=0, keepdims=True)
    i1 = jnp.min(jnp.where(in_grp == m1, rows, float(epg)), axis=0, keepdims=True)
    rest = jnp.where(rows == i1, -jnp.inf, in_grp)
    m2 = jnp.max(rest, axis=0, keepdims=True)
    i2 = jnp.min(jnp.where(rest == m2, rows, float(epg)), axis=0, keepdims=True)
    e21 = jnp.exp(m2 - m1)
    w1 = g_p / (1.0 + e21)
    w2 = g_p * e21 / (1.0 + e21)
    e1 = gidx * epg + i1.astype(I32)
    e2 = gidx * epg + i2.astype(I32)

    er = lax.broadcasted_iota(I32, (N_EXPERTS, tt), 0)
    oh1 = er == e1
    oh2 = er == e2
    cnt = jnp.where(oh1, 1.0, 0.0) + jnp.where(oh2, 1.0, 0.0)
    r = lax.broadcasted_iota(I32, (tt, tt), 0)
    c = lax.broadcasted_iota(I32, (tt, tt), 1)
    strict = jnp.where(r < c, 1.0, 0.0).astype(BF16)
    before = _dot(cnt.astype(BF16), strict) + carry_ref[:, 0:1]
    rank1 = jnp.sum(jnp.where(oh1, before, 0.0), axis=0, keepdims=True)
    rank2 = jnp.sum(jnp.where(oh2, before, 0.0), axis=0, keepdims=True)
    total = carry_ref[...] + jnp.sum(cnt, axis=1, keepdims=True)
    carry_ref[...] = total
    cnt_ref[...] = total

    o_ref[0:1, :] = e1.astype(F32)
    o_ref[1:2, :] = e2.astype(F32)
    o_ref[2:3, :] = rank1
    o_ref[3:4, :] = rank2
    o_ref[4:5, :] = w1
    o_ref[5:6, :] = w2
    o_ref[6:8, :] = jnp.zeros((2, tt), F32)


def _router(logits, bias_col):
    t = logits.shape[0]
    tt = ROUTER_TT
    return pl.pallas_call(
        _router_kernel,
        grid=(t // tt,),
        in_specs=[
            pl.BlockSpec((tt, LANES), lambda i: (i, 0)),
            pl.BlockSpec((LANES, LANES), lambda i: (0, 0)),
        ],
        out_specs=[
            pl.BlockSpec((SUBLANES, tt), lambda i: (0, i)),
            pl.BlockSpec((N_EXPERTS, LANES), lambda i: (0, 0)),
        ],
        out_shape=[jax.ShapeDtypeStruct((SUBLANES, t), F32), jax.ShapeDtypeStruct((N_EXPERTS, LANES), F32)],
        scratch_shapes=[pltpu.VMEM((N_EXPERTS, LANES), F32)],
        compiler_params=_params("arbitrary"),
        name="router",
    )(logits, bias_col)


TAB_EXPERT, TAB_VALID, TAB_FIRST, TAB_ORD, TAB_SLOT, TAB_NEXT, TAB_DONE, TAB_ROWS = range(8)


def _plan_kernel(r_ref, tot_ref, dest_ref, tab_ref):
    ne = N_EXPERTS
    blk = float(MOE_BLOCK)
    counts = tot_ref[...]
    nblk = jnp.floor((counts + (blk - 1.0)) * (1.0 / blk))
    padded = nblk * blk
    er = lax.broadcasted_iota(I32, (ne, ne), 0)
    ec = lax.broadcasted_iota(I32, (ne, ne), 1)
    tri = jnp.where(ec <= er, 1.0, 0.0)
    pad_ends = _dot_f32(tri, padded)
    pad_starts = pad_ends - padded
    ps_col = pad_starts[:, 0:1]

    tt = r_ref.shape[1]
    eid = lax.broadcasted_iota(I32, (ne, tt), 0).astype(F32)
    for k in range(2):
        start = jnp.sum(jnp.where(eid == r_ref[k:k + 1, :], ps_col, 0.0), axis=0, keepdims=True)
        dest_ref[k:k + 1, :] = (start + r_ref[k + 2:k + 3, :]).astype(I32)

    @pl.when(pl.program_id(0) == 0)
    def _():
        n_phases = float(W_CHUNKS // W_PHASE_CHUNKS)
        nonempty = jnp.where(nblk > 0.0, 1.0, 0.0)
        rank = _dot_f32(tri, nonempty) - 1.0
        slot_e = rank - 2.0 * jnp.floor(rank * 0.5)
        lane0 = jnp.where(lax.broadcasted_iota(I32, (SUBLANES, LANES), 1) == 0, 1.0, 0.0)
        ne_row = _dot_nt(lane0, nonempty)[0:1, :]
        nb_row = _dot_nt(lane0, nblk)[0:1, :]
        ecf = ec.astype(F32)
        cand = jnp.where(ec > er, jnp.where(ne_row > 0.0, ecf, float(ne)), float(ne))
        nxt_e = jnp.min(cand, axis=1, keepdims=True)
        nxt_e = jnp.where(nxt_e == float(ne), -1.0, nxt_e)
        cand = jnp.where(ec < er, jnp.where(ne_row > 0.0, ecf, -1.0), -1.0)
        prev_e = jnp.max(cand, axis=1, keepdims=True)
        nb_prev = jnp.sum(jnp.where(ecf == prev_e, nb_row, 0.0), axis=1, keepdims=True)
        done_e = jnp.where(prev_e >= 0.0, jnp.minimum(nb_prev, n_phases), 0.0)

        nbl = tab_ref.shape[1]
        brow = lax.broadcasted_iota(I32, (1, nbl), 1).astype(F32) * blk
        be = jnp.sum(jnp.where(brow >= pad_ends[:, 0:1], 1.0, 0.0), axis=0, keepdims=True)
        be = jnp.minimum(be, float(ne - 1))
        valid = jnp.where(brow < pad_ends[ne - 1:ne, 0:1], 1.0, 0.0)
        own = lax.broadcasted_iota(I32, (ne, nbl), 0).astype(F32) == be

        def pick(col):
            return jnp.sum(jnp.where(own, col, 0.0), axis=0, keepdims=True)

        ordinal = (brow - pick(ps_col)) * (1.0 / blk)
        first = valid * jnp.where(ordinal == 0.0, 1.0, 0.0)
        used = jnp.clip(pick(counts[:, 0:1]) - ordinal * blk, 0.0, blk)
        rows = {TAB_EXPERT: be, TAB_VALID: valid, TAB_FIRST: first, TAB_ORD: ordinal, TAB_SLOT: pick(slot_e[:, 0:1]),
                TAB_NEXT: pick(nxt_e), TAB_DONE: pick(done_e), TAB_ROWS: used}
        assert len(rows) == tab_ref.shape[0]
        for i, row in rows.items():
            tab_ref[i:i + 1, :] = row.astype(I32)


def _plan(routed, totals, n_blocks):
    t = routed.shape[1]
    tt = ROUTER_TT
    nbl = -(-n_blocks // LANES) * LANES
    return pl.pallas_call(
        _plan_kernel,
        grid=(t // tt,),
        in_specs=[
            pl.BlockSpec((SUBLANES, tt), lambda i: (0, i)),
            pl.BlockSpec((N_EXPERTS, LANES), lambda i: (0, 0)),
        ],
        out_specs=[
            pl.BlockSpec((2, tt), lambda i: (0, i)),
            pl.BlockSpec((SUBLANES, nbl), lambda i: (0, 0)),
        ],
        out_shape=[jax.ShapeDtypeStruct((2, t), I32), jax.ShapeDtypeStruct((SUBLANES, nbl), I32)],
        compiler_params=_params("arbitrary"),
        name="moe_plan",
    )(routed, totals)


def _row_copy(src_ref, src_row, dst_ref, dst_row, sem):
    return pltpu.make_async_copy(src_ref.at[pl.ds(src_row, 1), :], dst_ref.at[pl.ds(dst_row, 1), :], sem)


def _dispatch_kernel(d1_ref, d2_ref, tab_ref, h_ref, xs_ref, zero_ref, sem, zsem):
    tm = h_ref.shape[0]
    base = pl.program_id(0) * tm

    @pl.when(pl.program_id(0) == 0)
    def _():
        zero_ref[...] = jnp.zeros_like(zero_ref)

        def per_block(act):
            def body(b, carry):
                block_rows = xs_ref.at[pl.ds(pl.multiple_of(b * MOE_BLOCK, MOE_BLOCK), MOE_BLOCK), :]

                @pl.when(tab_ref[TAB_VALID, b] == 0)
                def _():
                    act(pltpu.make_async_copy(zero_ref, block_rows, zsem))

                @pl.when(tab_ref[TAB_VALID, b] != 0)
                def _():
                    used = tab_ref[TAB_ROWS, b]
                    head = (-used) & (SUBLANES - 1)
                    for r in range(SUBLANES - 1):
                        @pl.when(r < head)
                        def _():
                            act(_row_copy(zero_ref, 0, xs_ref, b * MOE_BLOCK + used + r, zsem))

                    rest = MOE_BLOCK - used - head
                    pos = b * MOE_BLOCK + used + head
                    for bit in range(SUBLANES.bit_length() - 1, MOE_BLOCK.bit_length() - 1):
                        size = 1 << bit

                        @pl.when((rest & size) != 0)
                        def _():
                            act(pltpu.make_async_copy(zero_ref.at[pl.ds(0, size), :],
                                                      xs_ref.at[pl.ds(pl.multiple_of(pos, SUBLANES), size), :], zsem))

                        pos = pos + (rest & size)

                return carry

            lax.fori_loop(0, xs_ref.shape[0] // MOE_BLOCK, body, 0)

        per_block(lambda cp: cp.start())
        per_block(lambda cp: cp.wait())

    def issue(g, carry):
        for k in range(ROW_UNROLL):
            r = g * ROW_UNROLL + k
            _row_copy(h_ref, r, xs_ref, d1_ref[base + r], sem).start()
            _row_copy(h_ref, r, xs_ref, d2_ref[base + r], sem).start()
        return carry

    lax.fori_loop(0, tm // ROW_UNROLL, issue, 0)

    def drain(g, carry):
        for k in range(ROW_UNROLL):
            r = g * ROW_UNROLL + k
            _row_copy(h_ref, r, xs_ref, 0, sem).wait()
            _row_copy(h_ref, r, xs_ref, 0, sem).wait()
        return carry

    lax.fori_loop(0, tm // ROW_UNROLL, drain, 0)


def _dispatch(dest1, dest2, tables, hn, p_rows):
    t, d = hn.shape
    tm = ROW_TM
    grid_spec = pltpu.PrefetchScalarGridSpec(
        num_scalar_prefetch=3,
        grid=(t // tm,),
        in_specs=[pl.BlockSpec((tm, d), lambda i, *_: (i, 0))],
        out_specs=pl.BlockSpec(memory_space=pl.ANY),
        scratch_shapes=[pltpu.VMEM((MOE_BLOCK, d), hn.dtype), pltpu.SemaphoreType.DMA, pltpu.SemaphoreType.DMA],
    )
    return pl.pallas_call(
        _dispatch_kernel,
        grid_spec=grid_spec,
        out_shape=jax.ShapeDtypeStruct((p_rows, d), hn.dtype),
        compiler_params=_params("arbitrary"),
        name="moe_dispatch",
    )(dest1, dest2, tables, hn)


def _weight_chunk(c, d, de):
    per = W_CHUNKS // 3
    if c < 2 * per:
        return c // per, (c % per) * (d // per), 0
    r, h = divmod(c - 2 * per, d // de)
    return 2, r * (d // per), h * de


def _moe_kernel(tab_ref, x_ref, wg_hbm, wu_hbm, wd_hbm, y_ref, wg_c, wu_c, wd_c, stage_ref, sems, *, layer):
    b = pl.program_id(0)
    expert = tab_ref[TAB_EXPERT, b]
    valid = tab_ref[TAB_VALID, b]
    first = tab_ref[TAB_FIRST, b]
    ordinal = tab_ref[TAB_ORD, b]
    slot = tab_ref[TAB_SLOT, b]
    nxt = tab_ref[TAB_NEXT, b]
    done = tab_ref[TAB_DONE, b]
    hbm = (wg_hbm, wu_hbm, wd_hbm)
    cache = (wg_c, wu_c, wd_c)
    d, de = wg_c.shape[1], wg_c.shape[2]
    rows, cols = stage_ref.shape[1], stage_ref.shape[2]

    sub = rows // W_CHUNK_SPLIT

    def chunk_copies(e, c, k):
        ti, r0, c0 = _weight_chunk(c, d, de)
        return [pltpu.make_async_copy(hbm[ti].at[layer, e, pl.ds(r0 + i * sub, sub), pl.ds(c0, cols)],
                                      stage_ref.at[k, pl.ds(i * sub, sub), :], sems.at[k])
                for i in range(W_CHUNK_SPLIT)]

    def start_phase(e, p):
        for k in range(W_PHASE_CHUNKS):
            for cp in chunk_copies(e, p * W_PHASE_CHUNKS + k, k):
                cp.start()

    def finish_phase(e, s, p):
        for k in range(W_PHASE_CHUNKS):
            c = p * W_PHASE_CHUNKS + k
            for cp in chunk_copies(e, c, k):
                cp.wait()
            ti, r0, c0 = _weight_chunk(c, d, de)
            cache[ti][s, pl.ds(r0, rows), pl.ds(c0, cols)] = stage_ref[k].astype(BF16)

    n_phases = W_CHUNKS // W_PHASE_CHUNKS

    @pl.when(valid != 0)
    def _():
        @pl.when(first != 0)
        def _():
            for p in range(n_phases):
                @pl.when(done <= p)
                def _():
                    start_phase(expert, p)
                    finish_phase(expert, slot, p)

        prefetch = (ordinal < n_phases) & (nxt >= 0)
        for p in range(n_phases):
            @pl.when(prefetch & (ordinal == p))
            def _():
                start_phase(nxt, p)

        x = _unpack_halves(x_ref[...])
        g = _dot(x, wg_c[slot])
        u = _dot(x, wu_c[slot])
        hid = (g * jax.nn.sigmoid(g) * u).astype(BF16)
        y_ref[...] = _pack_halves(_dot(hid, wd_c[slot]))

        for p in range(n_phases):
            @pl.when(prefetch & (ordinal == p))
            def _():
                finish_phase(nxt, 1 - slot, p)

    @pl.when(valid == 0)
    def _():
        y_ref[...] = jnp.zeros_like(y_ref)


def _moe(tables, xs, wg, wu, wd, layer):
    p_rows, dh = xs.shape
    tm = MOE_BLOCK
    d, de = wg.shape[2], wg.shape[3]
    assert d == 2 * dh and wd.shape[2:] == (de, d) and d % (W_CHUNKS // 3) == 0 and d % de == 0
    chunk = (d // (W_CHUNKS // 3), de)
    grid_spec = pltpu.PrefetchScalarGridSpec(
        num_scalar_prefetch=1,
        grid=(p_rows // tm,),
        in_specs=[
            pl.BlockSpec((tm, dh), lambda i, *_: (i, 0)),
            pl.BlockSpec(memory_space=pl.ANY),
            pl.BlockSpec(memory_space=pl.ANY),
            pl.BlockSpec(memory_space=pl.ANY),
        ],
        out_specs=pl.BlockSpec((tm, dh), lambda i, *_: (i, 0)),
        scratch_shapes=[
            pltpu.VMEM((2, d, de), BF16), pltpu.VMEM((2, d, de), BF16), pltpu.VMEM((2, de, d), BF16),
            pltpu.VMEM((W_PHASE_CHUNKS,) + chunk, F32), pltpu.SemaphoreType.DMA((W_PHASE_CHUNKS,)),
        ],
    )
    return pl.pallas_call(
        functools.partial(_moe_kernel, layer=layer),
        grid_spec=grid_spec,
        out_shape=jax.ShapeDtypeStruct((p_rows, dh), I32),
        compiler_params=_params("arbitrary"),
        name="moe_experts",
    )(tables, xs, wg, wu, wd)


def _combine_kernel(d1_ref, d2_ref, x_ref, w_ref, ys_ref, o_ref, buf, sems):
    tm = x_ref.shape[0]
    i = pl.program_id(0)
    slot = i % 2

    def gather(step, s, start):
        base = step * tm

        def body(g, carry):
            for k in range(ROW_UNROLL):
                r = g * ROW_UNROLL + k
                if start:
                    _row_copy(ys_ref, d1_ref[base + r], buf.at[s, 0], r, sems.at[s]).start()
                    _row_copy(ys_ref, d2_ref[base + r], buf.at[s, 1], r, sems.at[s]).start()
                else:
                    _row_copy(ys_ref, 0, buf.at[s, 0], r, sems.at[s]).wait()
                    _row_copy(ys_ref, 0, buf.at[s, 1], r, sems.at[s]).wait()
            return carry

        lax.fori_loop(0, tm // ROW_UNROLL, body, 0)

    @pl.when(i == 0)
    def _():
        gather(i, slot, True)

    @pl.when(i + 1 < pl.num_programs(0))
    def _():
        gather(i + 1, 1 - slot, True)

    gather(i, slot, False)
    w = w_ref[...]
    y1 = _unpack_halves(buf[slot, 0]).astype(F32)
    y2 = _unpack_halves(buf[slot, 1]).astype(F32)
    o_ref[...] = x_ref[...] + w[:, 0:1] * y1 + w[:, 1:2] * y2


def _combine(dest1, dest2, x, w12, ys):
    t, d = x.shape
    tm = ROW_TM
    dh = ys.shape[1]
    grid_spec = pltpu.PrefetchScalarGridSpec(
        num_scalar_prefetch=2,
        grid=(t // tm,),
        in_specs=[
            pl.BlockSpec((tm, d), lambda i, a, b: (i, 0)),
            pl.BlockSpec((tm, 2), lambda i, a, b: (i, 0)),
            pl.BlockSpec(memory_space=pl.ANY),
        ],
        out_specs=pl.BlockSpec((tm, d), lambda i, a, b: (i, 0)),
        scratch_shapes=[pltpu.VMEM((2, 2, tm, dh), I32), pltpu.SemaphoreType.DMA((2,))],
    )
    return pl.pallas_call(
        _combine_kernel,
        grid_spec=grid_spec,
        out_shape=jax.ShapeDtypeStruct((t, d), F32),
        compiler_params=_params("arbitrary"),
        name="moe_combine",
    )(dest1, dest2, x, w12, ys)


def _pad_cols(w, width):
    return jnp.pad(w, ((0, 0), (0, width - w.shape[1])))


def _moe_layer(x_mid, hn, logits, b_grp, b_exp, w_gate, w_up, w_down, layer):
    t, d = x_mid.shape
    bias = jnp.zeros((LANES,), F32)
    bias = bias.at[ROUTER_GROUP_ROW:ROUTER_GROUP_ROW + N_GROUPS].set(b_grp)
    bias = bias.at[ROUTER_EXPERT_ROW:ROUTER_EXPERT_ROW + N_EXPERTS].set(b_exp)
    routed, totals = _router(logits, jnp.broadcast_to(bias[:, None], (LANES, LANES)))
    p_rows = 2 * t + N_EXPERTS * MOE_BLOCK
    dest, tables = _plan(routed, totals, p_rows // MOE_BLOCK)
    dest1, dest2 = dest[0], dest[1]
    xs = _dispatch(dest1, dest2, tables, hn, p_rows)
    ys = _moe(tables, xs, w_gate, w_up, w_down, layer)
    w12 = jnp.stack([routed[4], routed[5]], axis=-1)
    return _combine(dest1, dest2, x_mid, w12, ys)


def _router_weight(w_grp, w_exp):
    d = w_grp.shape[0]
    wr = jnp.zeros((d, LANES), F32)
    wr = wr.at[:, ROUTER_GROUP_ROW:ROUTER_GROUP_ROW + N_GROUPS].set(w_grp)
    wr = wr.at[:, ROUTER_EXPERT_ROW:ROUTER_EXPERT_ROW + N_EXPERTS].set(w_exp)
    hi = wr.astype(BF16)
    lo = (wr - hi.astype(F32)).astype(BF16)
    return jnp.concatenate([hi, lo], axis=1)


def _fox_layer(x2, attn_g, w_in, b_f, q_g, k_g, mem_kv, memq_g, memk_g, batch, seq, mem_len):
    qkvo = 4 * FOX_W
    w_main = w_in[:, :qkvo].astype(BF16)
    side_w = -(-(w_in.shape[1] - qkvo) // LANES) * LANES
    w_side = _pad_cols(w_in[:, qkvo:], side_w).astype(BF16)
    scale = LOG2E / math.sqrt(HEAD_DIM)
    head_gain = jnp.concatenate([jnp.tile(q_g * scale, FOX_HEADS), jnp.tile(k_g, FOX_HEADS),
                                 jnp.ones((qkvo - 2 * FOX_W,), F32)]).reshape(1, -1)
    proj, side = _norm_matmul(x2, attn_g, w_main, head_gain, w_side, n_norm_tiles=2 * FOX_W // PROJ_TN,
                              tm=PROJ_TM, name="fox_in_proj")
    kb = _fox_decay(side, b_f, batch, seq)
    mix = _fox_attention(proj, kb, batch, seq)
    mo = _memory_attention(side, side_w, 0, FOX_HEADS, mem_kv, memq_g, memk_g, batch, seq, mem_len)
    return mix, mo


def _gla_layer(x2, attn_g, w_in, w_a2, b_a, o_g, mem_kv, memq_g, memk_g, batch, seq, mem_len):
    d = w_in.shape[0]
    kw = GLA_HEADS * GLA_DK
    vw = GLA_HEADS * GLA_DV
    pad = GLA_DKP - GLA_DK

    def pad_heads(w):
        lead = w.shape[0]
        return jnp.pad(w.reshape(lead, GLA_HEADS, GLA_DK), ((0, 0), (0, 0), (0, pad))).reshape(lead, -1)

    wq = pad_heads(w_in[:, :kw] * (1.0 / math.sqrt(GLA_DK)))
    wk = pad_heads(w_in[:, kw:2 * kw])
    rest = w_in[:, 2 * kw:2 * kw + 2 * vw]
    w_a1 = w_in[:, 2 * kw + 2 * vw:2 * kw + 2 * vw + GLA_RANK]
    w_qm = w_in[:, 2 * kw + 2 * vw + GLA_RANK:]
    w_main = jnp.concatenate([rest, wq, wk, w_qm], axis=1).astype(BF16)
    w_side = _pad_cols(w_a1, LANES).astype(BF16)
    head_gain = jnp.ones((1, w_main.shape[1]), F32)
    proj, a_side = _norm_matmul(x2, attn_g, w_main, head_gain, w_side, n_norm_tiles=0, tm=PROJ_TM,
                                name="gla_in_proj")
    wa = jnp.pad(pad_heads(w_a2), ((0, LANES - GLA_RANK), (0, 0)))
    ba = pad_heads(b_a.reshape(1, kw))
    mix = _gla(proj, a_side, wa, ba, o_g, batch, seq)
    q_col_block = (2 * GLA_HEADS * GLA_DKP + 2 * vw) // MEM_W
    mo = _memory_attention(proj, MEM_W, q_col_block, 0, mem_kv, memq_g, memk_g, batch, seq, mem_len)
    return mix, mo


def kernel(x, mem, mem_norm_g, w_mem_kv, attn_norm_g, fox_w_in, fox_b_f, fox_q_g, fox_k_g, gla_w_in, gla_w_a2,
           gla_b_a, gla_o_g, memq_g, memk_g, w_out, ffn_norm_g, w_grp, b_grp, w_exp, b_exp, w_gate, w_up, w_down):
    batch, seq, d = x.shape
    mem_len = mem.shape[1]
    depth = attn_norm_g.shape[0]
    x2 = x.reshape(batch * seq, d)

    n_kv = w_mem_kv.shape[1]
    mem_kv, _ = _norm_matmul(mem.reshape(batch * mem_len, d), mem_norm_g, w_mem_kv.astype(BF16),
                             jnp.ones((1, n_kv), F32), jnp.zeros((d, LANES), BF16), n_norm_tiles=0,
                             tm=batch * mem_len, name="mem_kv_proj")

    for i in range(depth):
        j = i // 2
        if i % 2 == 0:
            mix, mo = _fox_layer(x2, attn_norm_g[i], fox_w_in[j], fox_b_f[j], fox_q_g[j], fox_k_g[j],
                                 mem_kv, memq_g[i], memk_g[i], batch, seq, mem_len)
        else:
            mix, mo = _gla_layer(x2, attn_norm_g[i], gla_w_in[j], gla_w_a2[j], gla_b_a[j], gla_o_g[j],
                                 mem_kv, memq_g[i], memk_g[i], batch, seq, mem_len)
        w_o = w_out[i].astype(BF16)
        x_mid, hn, logits = _out_proj(mix, mo, w_o[:mix.shape[1]], w_o[mix.shape[1]:], x2, ffn_norm_g[i],
                                      _router_weight(w_grp[i], w_exp[i]))
        x2 = _moe_layer(x_mid, hn, logits, b_grp[i], b_exp[i], w_gate, w_up, w_down, i)
    return x2.reshape(batch, seq, d)
```

```python
import functools
import math

import jax
import jax.numpy as jnp
from jax import lax
from jax.experimental import pallas as pl
from jax.experimental.pallas import tpu as pltpu

F32 = jnp.float32
BF16 = jnp.bfloat16
I32 = jnp.int32

HEAD_DIM = 128
FOX_HEADS = 12
FOX_W = FOX_HEADS * HEAD_DIM
MEM_HEADS = 4
MEM_W = MEM_HEADS * HEAD_DIM
GLA_HEADS = 4
GLA_DV = FOX_W // GLA_HEADS
GLA_DK = GLA_DV // 2
GLA_DKP = 256
GLA_RANK = 16
GLA_TAU = 16.0
N_GROUPS = 4
EXPERTS_PER_GROUP = 8
N_EXPERTS = N_GROUPS * EXPERTS_PER_GROUP
RMS_EPS = 1e-6
LOG2E = 1.4426950408889634
GATE_GROUP = 16

LANES = 128
SUBLANES = 8
VMEM_LIMIT_BYTES = 56 * 1024 * 1024

PROJ_TM = 1024
PROJ_TN = 512
FOX_PROJ_TN = 1024
ATTN_T = 512
ATTN_HEADS_PER_STEP = 6
DECAY_TS = 512
MEM_TQ = 512
OUT_TM = 512
ROUTER_TT = 512
ROW_TM = 512
MOE_BLOCK = 256
ROW_UNROLL = 8
W_CHUNKS = 12
W_PHASE_CHUNKS = 3
W_CHUNK_SPLIT = 1
GLA_TS = 256
GLA_CHUNK = 64
ROUTER_GROUP_ROW = 0
ROUTER_EXPERT_ROW = 8
NEG_BIG = -1e30


def _params(*sem):
    return pltpu.CompilerParams(dimension_semantics=sem, vmem_limit_bytes=VMEM_LIMIT_BYTES)


def _log_sigmoid(z):
    return jnp.minimum(z, 0.0) - jnp.log1p(jnp.exp(-jnp.abs(z)))


def _dot(a, b):
    return jnp.dot(a, b, preferred_element_type=F32)


def _dot_nt(a, b):
    return lax.dot_general(a, b, (((1,), (1,)), ((), ())), preferred_element_type=F32)


def _dot_f32(a, b):
    return jnp.dot(a, b, preferred_element_type=F32, precision=lax.Precision.HIGHEST)


def _pack_halves(x):
    n = x.shape[1] // 2
    lo = lax.bitcast_convert_type(x[:, :n].astype(BF16).astype(F32), I32)
    hi = lax.bitcast_convert_type(x[:, n:].astype(BF16).astype(F32), I32)
    return lax.shift_right_logical(lo, 16) | hi


def _unpack_halves(w):
    lo = lax.bitcast_convert_type(lax.shift_left(w, 16), F32)
    hi = lax.bitcast_convert_type(w & jnp.int32(-65536), F32)
    return jnp.concatenate([lo.astype(BF16), hi.astype(BF16)], axis=1)


def _split3(x):
    hi = x.astype(BF16)
    r1 = x - hi.astype(F32)
    mid = r1.astype(BF16)
    lo = (r1 - mid.astype(F32)).astype(BF16)
    return hi, mid, lo


def _norm_matmul_kernel(x_ref, g_ref, w_ref, hg_ref, ws_ref, o_ref, os_ref, xn_ref, *, nj_main, n_norm_tiles):
    j = pl.program_id(1)

    @pl.when(j == 0)
    def _():
        x = x_ref[...]
        ms = jnp.mean(x * x, axis=-1, keepdims=True)
        xn_ref[...] = (x * lax.rsqrt(ms + RMS_EPS) * g_ref[...]).astype(BF16)

    if n_norm_tiles > 0:
        @pl.when(j < n_norm_tiles)
        def _():
            acc = _dot(xn_ref[...], w_ref[...])
            for h in range(acc.shape[1] // HEAD_DIM):
                sl = slice(h * HEAD_DIM, (h + 1) * HEAD_DIM)
                a = acc[:, sl]
                ms = jnp.mean(a * a, axis=-1, keepdims=True)
                o_ref[:, sl] = (a * lax.rsqrt(ms + RMS_EPS) * hg_ref[:, sl]).astype(o_ref.dtype)

    @pl.when((j >= n_norm_tiles) & (j < nj_main))
    def _():
        o_ref[...] = _dot(xn_ref[...], w_ref[...]).astype(o_ref.dtype)

    @pl.when(j == nj_main)
    def _():
        os_ref[...] = _dot(xn_ref[...], ws_ref[...])


def _norm_matmul(x, g, w_main, head_gain, w_side, *, n_norm_tiles, tm, name, tn=PROJ_TN):
    t, d = x.shape
    n_main = w_main.shape[1]
    nj_main = n_main // tn
    n_side = w_side.shape[1]
    assert t % tm == 0 and n_main % tn == 0 and n_side % LANES == 0
    last = nj_main - 1
    kernel = functools.partial(_norm_matmul_kernel, nj_main=nj_main, n_norm_tiles=n_norm_tiles)
    return pl.pallas_call(
        kernel,
        grid=(t // tm, nj_main + 1),
        in_specs=[
            pl.BlockSpec((tm, d), lambda i, j: (i, 0)),
            pl.BlockSpec((1, d), lambda i, j: (0, 0)),
            pl.BlockSpec((d, tn), lambda i, j: (0, jnp.minimum(j, last))),
            pl.BlockSpec((1, tn), lambda i, j: (0, jnp.minimum(j, last))),
            pl.BlockSpec((d, n_side), lambda i, j: (0, 0)),
        ],
        out_specs=[
            pl.BlockSpec((tm, tn), lambda i, j: (i, jnp.minimum(j, last))),
            pl.BlockSpec((tm, n_side), lambda i, j: (i, 0)),
        ],
        out_shape=[jax.ShapeDtypeStruct((t, n_main), BF16), jax.ShapeDtypeStruct((t, n_side), F32)],
        scratch_shapes=[pltpu.VMEM((tm, d), BF16)],
        compiler_params=_params("parallel", "arbitrary"),
        name=name,
    )(x, g.reshape(1, d), w_main, head_gain, w_side)


def _fox_decay_kernel(f_ref, b_ref, kb_ref, carry_ref):
    @pl.when(pl.program_id(1) == 0)
    def _():
        carry_ref[...] = jnp.zeros_like(carry_ref)

    lane = lax.broadcasted_iota(I32, f_ref.shape, 1)
    f = jnp.where(lane < FOX_HEADS, f_ref[...], 0.0)
    f = f + pltpu.roll(f, GATE_GROUP, 1) + pltpu.roll(f, 2 * GATE_GROUP, 1)
    lf = _log_sigmoid(f + b_ref[...]) * (-LOG2E)
    ts = lf.shape[0]
    r = lax.broadcasted_iota(I32, (ts, ts), 0)
    c = lax.broadcasted_iota(I32, (ts, ts), 1)
    incl = jnp.where(c <= r, 1.0, 0.0).astype(BF16)
    hi, mid, lo = _split3(lf)
    cs = _dot(incl, hi) + _dot(incl, mid) + _dot(incl, lo) + carry_ref[0:1, :]
    carry_ref[...] = jnp.broadcast_to(cs[ts - 1:ts, :], carry_ref.shape)
    hi, mid, lo = _split3(cs)
    piece = jnp.where(lane < GATE_GROUP, hi.astype(F32),
                      jnp.where(lane < 2 * GATE_GROUP, mid.astype(F32),
                                jnp.where(lane < 3 * GATE_GROUP, lo.astype(F32), 0.0)))
    kb_ref[...] = piece.astype(BF16)


def _fox_decay(f_side, col_block, b_f, batch, seq):
    ts = DECAY_TS
    ns = seq // ts
    b_pad = jnp.zeros((LANES,), F32)
    for p in range(3):
        b_pad = b_pad.at[p * GATE_GROUP:p * GATE_GROUP + FOX_HEADS].set(b_f)
    return pl.pallas_call(
        _fox_decay_kernel,
        grid=(batch, ns),
        in_specs=[
            pl.BlockSpec((ts, LANES), lambda b, s: (b * ns + s, col_block)),
            pl.BlockSpec((1, LANES), lambda b, s: (0, 0)),
        ],
        out_specs=pl.BlockSpec((ts, LANES), lambda b, s: (b * ns + s, 0)),
        out_shape=jax.ShapeDtypeStruct((batch * seq, LANES), BF16),
        scratch_shapes=[pltpu.VMEM((SUBLANES, LANES), F32)],
        compiler_params=_params("parallel", "arbitrary"),
        name="fox_decay",
    )(f_side, b_pad.reshape(1, LANES))


def _fox_attn_kernel(q_ref, k_ref, v_ref, kb_ref, og_ref, o_ref, m_ref, acc_ref):
    hp = pl.program_id(1)
    qi = pl.program_id(2)
    t = q_ref.shape[0]
    lane = lax.broadcasted_iota(I32, (t, LANES), 1)
    ones = jnp.ones((t, HEAD_DIM), BF16)
    q_aug = []
    for hh in range(ATTN_HEADS_PER_STEP):
        h = hp * ATTN_HEADS_PER_STEP + hh
        own = jnp.where(lane < 3 * GATE_GROUP, jnp.where((lane & (GATE_GROUP - 1)) == h, 1.0, 0.0), 0.0)
        q_aug.append(jnp.concatenate([q_ref[:, hh * HEAD_DIM:(hh + 1) * HEAD_DIM], own.astype(BF16)], axis=1))
    m_ref[...] = jnp.full_like(m_ref, NEG_BIG)
    acc_ref[...] = jnp.zeros_like(acc_ref)

    def tile(j, masked):
        rows = pl.ds(pl.multiple_of(j * t, t), t)
        kb = kb_ref[rows, :]
        for hh in range(ATTN_HEADS_PER_STEP):
            cols = slice(hh * HEAD_DIM, (hh + 1) * HEAD_DIM)
            k_aug = jnp.concatenate([k_ref[rows, cols], kb], axis=1)
            s = _dot_nt(q_aug[hh], k_aug)
            if masked:
                row = lax.broadcasted_iota(I32, (t, t), 0)
                col = lax.broadcasted_iota(I32, (t, t), 1)
                s = jnp.where(col <= row, s, NEG_BIG)
            m_prev = m_ref[hh]
            m_new = jnp.maximum(m_prev, jnp.max(s, axis=-1, keepdims=True))
            alpha = jnp.exp2(m_prev - m_new)
            p = jnp.exp2(s - jnp.concatenate([m_new] * (t // LANES), axis=1))
            v_aug = jnp.concatenate([v_ref[rows, cols], ones], axis=1)
            pv = _dot(p.astype(BF16), v_aug)
            acc_ref[hh] = jnp.concatenate([alpha, alpha], axis=1) * acc_ref[hh] + pv
            m_ref[hh] = m_new

    def full_tile(j, carry):
        tile(j, False)
        return carry

    lax.fori_loop(0, qi, full_tile, 0)
    tile(qi, True)
    for hh in range(ATTN_HEADS_PER_STEP):
        cols = slice(hh * HEAD_DIM, (hh + 1) * HEAD_DIM)
        acc = acc_ref[hh]
        og = og_ref[:, cols].astype(F32)
        o_ref[:, cols] = (acc[:, :HEAD_DIM] / acc[:, HEAD_DIM:] * jax.nn.sigmoid(og)).astype(o_ref.dtype)


def _fox_attention(proj, kb, batch, seq):
    t = ATTN_T
    nq = seq // t
    hps = ATTN_HEADS_PER_STEP
    w = hps * HEAD_DIM
    nhp = FOX_HEADS // hps
    return pl.pallas_call(
        _fox_attn_kernel,
        grid=(batch, nhp, nq),
        in_specs=[
            pl.BlockSpec((t, w), lambda b, h, i: (b * nq + i, h)),
            pl.BlockSpec((seq, w), lambda b, h, i: (b, nhp + h)),
            pl.BlockSpec((seq, w), lambda b, h, i: (b, 2 * nhp + h)),
            pl.BlockSpec((seq, LANES), lambda b, h, i: (b, 0)),
            pl.BlockSpec((t, w), lambda b, h, i: (b * nq + i, 3 * nhp + h)),
        ],
        out_specs=pl.BlockSpec((t, w), lambda b, h, i: (b * nq + i, h)),
        out_shape=jax.ShapeDtypeStruct((batch * seq, FOX_W), BF16),
        scratch_shapes=[pltpu.VMEM((hps, t, LANES), F32), pltpu.VMEM((hps, t, 2 * HEAD_DIM), F32)],
        compiler_params=_params("parallel", "parallel", "arbitrary"),
        name="fox_attention",
    )(proj, proj, proj, kb, proj)


def _mem_attn_kernel(q_ref, k_ref, v_ref, qg_ref, kg_ref, o_ref, *, q_offset):
    for h in range(MEM_HEADS):
        sl = slice(h * HEAD_DIM, (h + 1) * HEAD_DIM)
        q = q_ref[:, q_offset + h * HEAD_DIM:q_offset + (h + 1) * HEAD_DIM].astype(F32)
        q = q * lax.rsqrt(jnp.mean(q * q, axis=-1, keepdims=True) + RMS_EPS) * qg_ref[...]
        k = k_ref[:, sl].astype(F32)
        k = k * lax.rsqrt(jnp.mean(k * k, axis=-1, keepdims=True) + RMS_EPS) * kg_ref[...]
        s = _dot_nt(q.astype(BF16), k.astype(BF16))
        p = jnp.exp(s - jnp.max(s, axis=-1, keepdims=True))
        o = _dot(p.astype(BF16), v_ref[:, sl]) / jnp.sum(p, axis=-1, keepdims=True)
        o_ref[:, sl] = o.astype(o_ref.dtype)


def _memory_attention(q_src, q_block_w, q_col_block, q_offset, mem_kv, q_gain, k_gain, batch, seq, mem_len):
    tq = MEM_TQ
    nq = seq // tq
    scale = 1.0 / math.sqrt(HEAD_DIM)
    return pl.pallas_call(
        functools.partial(_mem_attn_kernel, q_offset=q_offset),
        grid=(batch * nq,),
        in_specs=[
            pl.BlockSpec((tq, q_block_w), lambda i: (i, q_col_block)),
            pl.BlockSpec((mem_len, MEM_W), lambda i: (i // nq, 0)),
            pl.BlockSpec((mem_len, MEM_W), lambda i: (i // nq, 1)),
            pl.BlockSpec((1, HEAD_DIM), lambda i: (0, 0)),
            pl.BlockSpec((1, HEAD_DIM), lambda i: (0, 0)),
        ],
        out_specs=pl.BlockSpec((tq, MEM_W), lambda i: (i, 0)),
        out_shape=jax.ShapeDtypeStruct((batch * seq, MEM_W), BF16),
        compiler_params=_params("parallel"),
        name="memory_attention",
    )(q_src, mem_kv, mem_kv, (q_gain * scale).reshape(1, HEAD_DIM), k_gain.reshape(1, HEAD_DIM))


def _gla_kernel(q_ref, k_ref, v_ref, r_ref, a_ref, wa_ref, ba_ref, og_ref, o_ref,
                st_ref, qe_ref, q2_ref, k2_ref, kd_ref, oi_ref):
    @pl.when(pl.program_id(1) == 0)
    def _():
        st_ref[...] = jnp.zeros_like(st_ref)

    ts = q_ref.shape[0]
    c = GLA_CHUNK
    shift = c.bit_length() - 1
    n_chunks = ts // c

    z = _dot_f32(a_ref[...], wa_ref[...]) + ba_ref[...]
    g = _log_sigmoid(z) * (1.0 / GLA_TAU)
    ri = lax.broadcasted_iota(I32, (ts, ts), 0)
    ci = lax.broadcasted_iota(I32, (ts, ts), 1)
    chunk_start = lax.shift_left(lax.shift_right_logical(ri, shift), shift)

    def chunk_causal(val):
        return jnp.where(ci <= ri, jnp.where(ci >= chunk_start, val, 0.0), 0.0)

    incl = chunk_causal(jnp.ones((ts, ts), F32)).astype(BF16)
    hi, mid, lo = _split3(g)
    bc = _dot(incl, hi) + _dot(incl, mid) + _dot(incl, lo)
    decay = []
    for ic in range(n_chunks):
        rows = slice(ic * c, (ic + 1) * c)
        bcc = bc[rows]
        b_mid = bcc[c // 2 - 1:c // 2]
        b_last = bcc[c - 1:c]
        q = q_ref[rows, :].astype(F32)
        k = k_ref[rows, :].astype(F32)
        qe_ref[rows, :] = (q * jnp.exp(bcc)).astype(BF16)
        q2_ref[rows, :] = (q * jnp.exp(bcc - b_mid)).astype(BF16)
        k2_ref[rows, :] = (k * jnp.exp(b_mid - bcc)).astype(BF16)
        kd_ref[rows, :] = (k * jnp.exp(b_last - bcc)).astype(BF16)
        decay.append(jnp.exp(b_last))
    for h in range(GLA_HEADS):
        ks = slice(h * GLA_DKP, (h + 1) * GLA_DKP)
        vs = slice(h * GLA_DV, (h + 1) * GLA_DV)
        att = chunk_causal(_dot_nt(q2_ref[:, ks], k2_ref[:, ks]))
        oi_ref[:, vs] = _dot(att.astype(BF16), v_ref[:, vs])

    for ic in range(n_chunks):
        rows = slice(ic * c, (ic + 1) * c)
        for h in range(GLA_HEADS):
            ks = slice(h * GLA_DKP, (h + 1) * GLA_DKP)
            vs = slice(h * GLA_DV, (h + 1) * GLA_DV)
            st = st_ref[h]
            o = _dot_nt(qe_ref[rows, ks], st.astype(BF16)) + oi_ref[rows, vs]
            upd = lax.dot_general(v_ref[rows, vs], kd_ref[rows, ks], (((0,), (0,)), ((), ())),
                                  preferred_element_type=F32)
            st_ref[h] = st * decay[ic][:, ks] + upd
            on = o * lax.rsqrt(jnp.mean(o * o, axis=-1, keepdims=True) + RMS_EPS) * og_ref[...]
            rh = r_ref[rows, vs].astype(F32)
            o_ref[rows, vs] = (on * rh * jax.nn.sigmoid(rh)).astype(o_ref.dtype)


def _gla(proj, a_side, wa, ba, o_gain, batch, seq):
    ts = GLA_TS
    ns = seq // ts
    kw = GLA_HEADS * GLA_DKP
    vw = GLA_HEADS * GLA_DV
    assert (2 * vw) % kw == 0
    q_blk = (2 * vw) // kw
    return pl.pallas_call(
        _gla_kernel,
        grid=(batch, ns),
        in_specs=[
            pl.BlockSpec((ts, kw), lambda b, s: (b * ns + s, q_blk)),
            pl.BlockSpec((ts, kw), lambda b, s: (b * ns + s, q_blk + 1)),
            pl.BlockSpec((ts, vw), lambda b, s: (b * ns + s, 0)),
            pl.BlockSpec((ts, vw), lambda b, s: (b * ns + s, 1)),
            pl.BlockSpec((ts, LANES), lambda b, s: (b * ns + s, 0)),
            pl.BlockSpec((LANES, kw), lambda b, s: (0, 0)),
            pl.BlockSpec((1, kw), lambda b, s: (0, 0)),
            pl.BlockSpec((1, GLA_DV), lambda b, s: (0, 0)),
        ],
        out_specs=pl.BlockSpec((ts, vw), lambda b, s: (b * ns + s, 0)),
        out_shape=jax.ShapeDtypeStruct((batch * seq, vw), BF16),
        scratch_shapes=[pltpu.VMEM((GLA_HEADS, GLA_DV, GLA_DKP), F32)] + [pltpu.VMEM((ts, kw), BF16)] * 4
        + [pltpu.VMEM((ts, vw), F32)],
        compiler_params=_params("parallel", "arbitrary"),
        name="gla",
    )(proj, proj, proj, proj, a_side, wa, ba, o_gain.reshape(1, GLA_DV))


def _out_proj_kernel(mix_ref, mo_ref, w1_ref, w2_ref, x_ref, g_ref, wr_ref, xo_ref, hn_ref, lg_ref):
    y = _dot(mix_ref[...], w1_ref[...]) + _dot(mo_ref[...], w2_ref[...])
    xn = x_ref[...] + y
    xo_ref[...] = xn
    hn = xn * lax.rsqrt(jnp.mean(xn * xn, axis=-1, keepdims=True) + RMS_EPS) * g_ref[...]
    hn_ref[...] = _pack_halves(hn)
    h_hi = hn.astype(BF16)
    h_lo = (hn - h_hi.astype(F32)).astype(BF16)
    both = _dot(h_hi, wr_ref[...])
    lg_ref[...] = both[:, :LANES] + both[:, LANES:] + _dot(h_lo, wr_ref[:, :LANES])


def _out_proj(mix, mo, w1, w2, x, g, wr):
    t, d = x.shape
    tm = OUT_TM
    once = pl.Buffered(1)
    return pl.pallas_call(
        _out_proj_kernel,
        grid=(t // tm,),
        in_specs=[
            pl.BlockSpec((tm, mix.shape[1]), lambda i: (i, 0)),
            pl.BlockSpec((tm, mo.shape[1]), lambda i: (i, 0)),
            pl.BlockSpec(w1.shape, lambda i: (0, 0), pipeline_mode=once),
            pl.BlockSpec(w2.shape, lambda i: (0, 0), pipeline_mode=once),
            pl.BlockSpec((tm, d), lambda i: (i, 0)),
            pl.BlockSpec((1, d), lambda i: (0, 0), pipeline_mode=once),
            pl.BlockSpec((d, 2 * LANES), lambda i: (0, 0), pipeline_mode=once),
        ],
        out_specs=[
            pl.BlockSpec((tm, d), lambda i: (i, 0)),
            pl.BlockSpec((tm, d // 2), lambda i: (i, 0)),
            pl.BlockSpec((tm, LANES), lambda i: (i, 0)),
        ],
        out_shape=[jax.ShapeDtypeStruct((t, d), F32), jax.ShapeDtypeStruct((t, d // 2), I32),
                   jax.ShapeDtypeStruct((t, LANES), F32)],
        compiler_params=_params("parallel"),
        name="out_proj",
    )(mix, mo, w1, w2, x, g.reshape(1, d), wr)


def _router_kernel(lg_ref, b_ref, o_ref, cnt_ref, carry_ref):
    @pl.when(pl.program_id(0) == 0)
    def _():
        carry_ref[...] = jnp.zeros_like(carry_ref)

    logits = lg_ref[...].T + b_ref[:, 0:1]
    tt = logits.shape[1]
    gl = [logits[ROUTER_GROUP_ROW + g:ROUTER_GROUP_ROW + g + 1, :] for g in range(N_GROUPS)]
    best = gl[0]
    gidx = jnp.zeros((1, tt), I32)
    for g in range(1, N_GROUPS):
        better = gl[g] > best
        best = jnp.where(better, gl[g], best)
        gidx = jnp.where(better, g, gidx)
    denom = jnp.exp(gl[0] - best)
    for g in range(1, N_GROUPS):
        denom = denom + jnp.exp(gl[g] - best)
    g_p = 1.0 / denom

    epg = EXPERTS_PER_GROUP
    in_grp = logits[ROUTER_EXPERT_ROW:ROUTER_EXPERT_ROW + epg, :]
    for g in range(1, N_GROUPS):
        lo = ROUTER_EXPERT_ROW + g * epg
        in_grp = jnp.where(gidx == g, logits[lo:lo + epg, :], in_grp)
    rows = lax.broadcasted_iota(I32, (epg, tt), 0).astype(F32)
    m1 = jnp.max(in_grp, axis=0, keepdims=True)
    i1 = jnp.min(jnp.where(in_grp == m1, rows, float(epg)), axis=0, keepdims=True)
    rest = jnp.where(rows == i1, -jnp.inf, in_grp)
    m2 = jnp.max(rest, axis=0, keepdims=True)
    i2 = jnp.min(jnp.where(rest == m2, rows, float(epg)), axis=0, keepdims=True)
    e21 = jnp.exp(m2 - m1)
    w1 = g_p / (1.0 + e21)
    w2 = g_p * e21 / (1.0 + e21)
    e1 = gidx * epg + i1.astype(I32)
    e2 = gidx * epg + i2.astype(I32)

    er = lax.broadcasted_iota(I32, (N_EXPERTS, tt), 0)
    oh1 = er == e1
    oh2 = er == e2
    cnt = jnp.where(oh1, 1.0, 0.0) + jnp.where(oh2, 1.0, 0.0)
    r = lax.broadcasted_iota(I32, (tt, tt), 0)
    c = lax.broadcasted_iota(I32, (tt, tt), 1)
    strict = jnp.where(r < c, 1.0, 0.0).astype(BF16)
    before = _dot(cnt.astype(BF16), strict) + carry_ref[:, 0:1]
    rank1 = jnp.sum(jnp.where(oh1, before, 0.0), axis=0, keepdims=True)
    rank2 = jnp.sum(jnp.where(oh2, before, 0.0), axis=0, keepdims=True)
    total = carry_ref[...] + jnp.sum(cnt, axis=1, keepdims=True)
    carry_ref[...] = total
    cnt_ref[...] = total

    o_ref[0:1, :] = e1.astype(F32)
    o_ref[1:2, :] = e2.astype(F32)
    o_ref[2:3, :] = rank1
    o_ref[3:4, :] = rank2
    o_ref[4:5, :] = w1
    o_ref[5:6, :] = w2
    o_ref[6:8, :] = jnp.zeros((2, tt), F32)


def _router(logits, bias_col):
    t = logits.shape[0]
    tt = ROUTER_TT
    return pl.pallas_call(
        _router_kernel,
        grid=(t // tt,),
        in_specs=[
            pl.BlockSpec((tt, LANES), lambda i: (i, 0)),
            pl.BlockSpec((LANES, LANES), lambda i: (0, 0)),
        ],
        out_specs=[
            pl.BlockSpec((SUBLANES, tt), lambda i: (0, i)),
            pl.BlockSpec((N_EXPERTS, LANES), lambda i: (0, 0)),
        ],
        out_shape=[jax.ShapeDtypeStruct((SUBLANES, t), F32), jax.ShapeDtypeStruct((N_EXPERTS, LANES), F32)],
        scratch_shapes=[pltpu.VMEM((N_EXPERTS, LANES), F32)],
        compiler_params=_params("arbitrary"),
        name="router",
    )(logits, bias_col)


TAB_EXPERT, TAB_VALID, TAB_FIRST, TAB_ORD, TAB_SLOT, TAB_NEXT, TAB_DONE, TAB_ROWS = range(8)


def _plan_kernel(r_ref, tot_ref, dest_ref, tab_ref):
    ne = N_EXPERTS
    blk = float(MOE_BLOCK)
    counts = tot_ref[...]
    nblk = jnp.floor((counts + (blk - 1.0)) * (1.0 / blk))
    padded = nblk * blk
    er = lax.broadcasted_iota(I32, (ne, ne), 0)
    ec = lax.broadcasted_iota(I32, (ne, ne), 1)
    tri = jnp.where(ec <= er, 1.0, 0.0)
    pad_ends = _dot_f32(tri, padded)
    pad_starts = pad_ends - padded
    ps_col = pad_starts[:, 0:1]

    tt = r_ref.shape[1]
    eid = lax.broadcasted_iota(I32, (ne, tt), 0).astype(F32)
    for k in range(2):
        start = jnp.sum(jnp.where(eid == r_ref[k:k + 1, :], ps_col, 0.0), axis=0, keepdims=True)
        dest_ref[k:k + 1, :] = (start + r_ref[k + 2:k + 3, :]).astype(I32)

    @pl.when(pl.program_id(0) == 0)
    def _():
        n_phases = float(W_CHUNKS // W_PHASE_CHUNKS)
        nonempty = jnp.where(nblk > 0.0, 1.0, 0.0)
        rank = _dot_f32(tri, nonempty) - 1.0
        slot_e = rank - 2.0 * jnp.floor(rank * 0.5)
        lane0 = jnp.where(lax.broadcasted_iota(I32, (SUBLANES, LANES), 1) == 0, 1.0, 0.0)
        ne_row = _dot_nt(lane0, nonempty)[0:1, :]
        nb_row = _dot_nt(lane0, nblk)[0:1, :]
        ecf = ec.astype(F32)
        cand = jnp.where(ec > er, jnp.where(ne_row > 0.0, ecf, float(ne)), float(ne))
        nxt_e = jnp.min(cand, axis=1, keepdims=True)
        nxt_e = jnp.where(nxt_e == float(ne), -1.0, nxt_e)
        cand = jnp.where(ec < er, jnp.where(ne_row > 0.0, ecf, -1.0), -1.0)
        prev_e = jnp.max(cand, axis=1, keepdims=True)
        nb_prev = jnp.sum(jnp.where(ecf == prev_e, nb_row, 0.0), axis=1, keepdims=True)
        done_e = jnp.where(prev_e >= 0.0, jnp.minimum(nb_prev, n_phases), 0.0)

        nbl = tab_ref.shape[1]
        brow = lax.broadcasted_iota(I32, (1, nbl), 1).astype(F32) * blk
        be = jnp.sum(jnp.where(brow >= pad_ends[:, 0:1], 1.0, 0.0), axis=0, keepdims=True)
        be = jnp.minimum(be, float(ne - 1))
        valid = jnp.where(brow < pad_ends[ne - 1:ne, 0:1], 1.0, 0.0)
        own = lax.broadcasted_iota(I32, (ne, nbl), 0).astype(F32) == be

        def pick(col):
            return jnp.sum(jnp.where(own, col, 0.0), axis=0, keepdims=True)

        ordinal = (brow - pick(ps_col)) * (1.0 / blk)
        first = valid * jnp.where(ordinal == 0.0, 1.0, 0.0)
        used = jnp.clip(pick(counts[:, 0:1]) - ordinal * blk, 0.0, blk)
        rows = {TAB_EXPERT: be, TAB_VALID: valid, TAB_FIRST: first, TAB_ORD: ordinal, TAB_SLOT: pick(slot_e[:, 0:1]),
                TAB_NEXT: pick(nxt_e), TAB_DONE: pick(done_e), TAB_ROWS: used}
        assert len(rows) == tab_ref.shape[0]
        for i, row in rows.items():
            tab_ref[i:i + 1, :] = row.astype(I32)


def _plan(routed, totals, n_blocks):
    t = routed.shape[1]
    tt = ROUTER_TT
    nbl = -(-n_blocks // LANES) * LANES
    return pl.pallas_call(
        _plan_kernel,
        grid=(t // tt,),
        in_specs=[
            pl.BlockSpec((SUBLANES, tt), lambda i: (0, i)),
            pl.BlockSpec((N_EXPERTS, LANES), lambda i: (0, 0)),
        ],
        out_specs=[
            pl.BlockSpec((2, tt), lambda i: (0, i)),
            pl.BlockSpec((SUBLANES, nbl), lambda i: (0, 0)),
        ],
        out_shape=[jax.ShapeDtypeStruct((2, t), I32), jax.ShapeDtypeStruct((SUBLANES, nbl), I32)],
        compiler_params=_params("arbitrary"),
        name="moe_plan",
    )(routed, totals)


def _row_copy(src_ref, src_row, dst_ref, dst_row, sem):
    return pltpu.make_async_copy(src_ref.at[pl.ds(src_row, 1), :], dst_ref.at[pl.ds(dst_row, 1), :], sem)


def _dispatch_kernel(d1_ref, d2_ref, tab_ref, h_ref, xs_ref, zero_ref, sem, zsem):
    tm = h_ref.shape[0]
    base = pl.program_id(0) * tm

    @pl.when(pl.program_id(0) == 0)
    def _():
        zero_ref[...] = jnp.zeros_like(zero_ref)

        def per_block(act):
            def body(b, carry):
                block_rows = xs_ref.at[pl.ds(pl.multiple_of(b * MOE_BLOCK, MOE_BLOCK), MOE_BLOCK), :]

                @pl.when(tab_ref[TAB_VALID, b] == 0)
                def _():
                    act(pltpu.make_async_copy(zero_ref, block_rows, zsem))

                @pl.when(tab_ref[TAB_VALID, b] != 0)
                def _():
                    used = tab_ref[TAB_ROWS, b]
                    head = (-used) & (SUBLANES - 1)
                    for r in range(SUBLANES - 1):
                        @pl.when(r < head)
                        def _():
                            act(_row_copy(zero_ref, 0, xs_ref, b * MOE_BLOCK + used + r, zsem))

                    rest = MOE_BLOCK - used - head
                    pos = b * MOE_BLOCK + used + head
                    for bit in range(SUBLANES.bit_length() - 1, MOE_BLOCK.bit_length() - 1):
                        size = 1 << bit

                        @pl.when((rest & size) != 0)
                        def _():
                            act(pltpu.make_async_copy(zero_ref.at[pl.ds(0, size), :],
                                                      xs_ref.at[pl.ds(pl.multiple_of(pos, SUBLANES), size), :], zsem))

                        pos = pos + (rest & size)

                return carry

            lax.fori_loop(0, xs_ref.shape[0] // MOE_BLOCK, body, 0)

        per_block(lambda cp: cp.start())
        per_block(lambda cp: cp.wait())

    def issue(g, carry):
        for k in range(ROW_UNROLL):
            r = g * ROW_UNROLL + k
            _row_copy(h_ref, r, xs_ref, d1_ref[base + r], sem).start()
            _row_copy(h_ref, r, xs_ref, d2_ref[base + r], sem).start()
        return carry

    lax.fori_loop(0, tm // ROW_UNROLL, issue, 0)

    def drain(g, carry):
        for k in range(ROW_UNROLL):
            r = g * ROW_UNROLL + k
            _row_copy(h_ref, r, xs_ref, 0, sem).wait()
            _row_copy(h_ref, r, xs_ref, 0, sem).wait()
        return carry

    lax.fori_loop(0, tm // ROW_UNROLL, drain, 0)


def _dispatch(dest1, dest2, tables, hn, p_rows):
    t, d = hn.shape
    tm = ROW_TM
    grid_spec = pltpu.PrefetchScalarGridSpec(
        num_scalar_prefetch=3,
        grid=(t // tm,),
        in_specs=[pl.BlockSpec((tm, d), lambda i, *_: (i, 0))],
        out_specs=pl.BlockSpec(memory_space=pl.ANY),
        scratch_shapes=[pltpu.VMEM((MOE_BLOCK, d), hn.dtype), pltpu.SemaphoreType.DMA, pltpu.SemaphoreType.DMA],
    )
    return pl.pallas_call(
        _dispatch_kernel,
        grid_spec=grid_spec,
        out_shape=jax.ShapeDtypeStruct((p_rows, d), hn.dtype),
        compiler_params=_params("arbitrary"),
        name="moe_dispatch",
    )(dest1, dest2, tables, hn)


def _weight_chunk(c, d, de):
    per = W_CHUNKS // 3
    if c < 2 * per:
        return c // per, (c % per) * (d // per), 0
    r, h = divmod(c - 2 * per, d // de)
    return 2, r * (d // per), h * de


def _moe_kernel(tab_ref, x_ref, wg_hbm, wu_hbm, wd_hbm, y_ref, wg_c, wu_c, wd_c, stage_ref, sems, *, layer):
    b = pl.program_id(0)
    expert = tab_ref[TAB_EXPERT, b]
    valid = tab_ref[TAB_VALID, b]
    first = tab_ref[TAB_FIRST, b]
    ordinal = tab_ref[TAB_ORD, b]
    slot = tab_ref[TAB_SLOT, b]
    nxt = tab_ref[TAB_NEXT, b]
    done = tab_ref[TAB_DONE, b]
    hbm = (wg_hbm, wu_hbm, wd_hbm)
    cache = (wg_c, wu_c, wd_c)
    d, de = wg_c.shape[1], wg_c.shape[2]
    rows, cols = stage_ref.shape[1], stage_ref.shape[2]

    sub = rows // W_CHUNK_SPLIT

    def chunk_copies(e, c, k):
        ti, r0, c0 = _weight_chunk(c, d, de)
        return [pltpu.make_async_copy(hbm[ti].at[layer, e, pl.ds(r0 + i * sub, sub), pl.ds(c0, cols)],
                                      stage_ref.at[k, pl.ds(i * sub, sub), :], sems.at[k])
                for i in range(W_CHUNK_SPLIT)]

    def start_phase(e, p):
        for k in range(W_PHASE_CHUNKS):
            for cp in chunk_copies(e, p * W_PHASE_CHUNKS + k, k):
                cp.start()

    def finish_phase(e, s, p):
        for k in range(W_PHASE_CHUNKS):
            c = p * W_PHASE_CHUNKS + k
            for cp in chunk_copies(e, c, k):
                cp.wait()
            ti, r0, c0 = _weight_chunk(c, d, de)
            cache[ti][s, pl.ds(r0, rows), pl.ds(c0, cols)] = stage_ref[k].astype(BF16)

    n_phases = W_CHUNKS // W_PHASE_CHUNKS

    @pl.when(valid != 0)
    def _():
        @pl.when(first != 0)
        def _():
            for p in range(n_phases):
                @pl.when(done <= p)
                def _():
                    start_phase(expert, p)
                    finish_phase(expert, slot, p)

        prefetch = (ordinal < n_phases) & (nxt >= 0)
        for p in range(n_phases):
            @pl.when(prefetch & (ordinal == p))
            def _():
                start_phase(nxt, p)

        x = _unpack_halves(x_ref[...])
        g = _dot(x, wg_c[slot])
        u = _dot(x, wu_c[slot])
        hid = (g * jax.nn.sigmoid(g) * u).astype(BF16)
        y_ref[...] = _pack_halves(_dot(hid, wd_c[slot]))

        for p in range(n_phases):
            @pl.when(prefetch & (ordinal == p))
            def _():
                finish_phase(nxt, 1 - slot, p)

    @pl.when(valid == 0)
    def _():
        y_ref[...] = jnp.zeros_like(y_ref)


def _moe(tables, xs, wg, wu, wd, layer):
    p_rows, dh = xs.shape
    tm = MOE_BLOCK
    d, de = wg.shape[2], wg.shape[3]
    assert d == 2 * dh and wd.shape[2:] == (de, d) and d % (W_CHUNKS // 3) == 0 and d % de == 0
    chunk = (d // (W_CHUNKS // 3), de)
    grid_spec = pltpu.PrefetchScalarGridSpec(
        num_scalar_prefetch=1,
        grid=(p_rows // tm,),
        in_specs=[
            pl.BlockSpec((tm, dh), lambda i, *_: (i, 0)),
            pl.BlockSpec(memory_space=pl.ANY),
            pl.BlockSpec(memory_space=pl.ANY),
            pl.BlockSpec(memory_space=pl.ANY),
        ],
        out_specs=pl.BlockSpec((tm, dh), lambda i, *_: (i, 0)),
        scratch_shapes=[
            pltpu.VMEM((2, d, de), BF16), pltpu.VMEM((2, d, de), BF16), pltpu.VMEM((2, de, d), BF16),
            pltpu.VMEM((W_PHASE_CHUNKS,) + chunk, F32), pltpu.SemaphoreType.DMA((W_PHASE_CHUNKS,)),
        ],
    )
    return pl.pallas_call(
        functools.partial(_moe_kernel, layer=layer),
        grid_spec=grid_spec,
        out_shape=jax.ShapeDtypeStruct((p_rows, dh), I32),
        compiler_params=_params("arbitrary"),
        name="moe_experts",
    )(tables, xs, wg, wu, wd)


def _combine_kernel(d1_ref, d2_ref, x_ref, w_ref, ys_ref, o_ref, buf, sems):
    tm = x_ref.shape[0]
    i = pl.program_id(0)
    slot = i % 2

    def gather(step, s, start):
        base = step * tm

        def body(g, carry):
            for k in range(ROW_UNROLL):
                r = g * ROW_UNROLL + k
                if start:
                    _row_copy(ys_ref, d1_ref[base + r], buf.at[s, 0], r, sems.at[s]).start()
                    _row_copy(ys_ref, d2_ref[base + r], buf.at[s, 1], r, sems.at[s]).start()
                else:
                    _row_copy(ys_ref, 0, buf.at[s, 0], r, sems.at[s]).wait()
                    _row_copy(ys_ref, 0, buf.at[s, 1], r, sems.at[s]).wait()
            return carry

        lax.fori_loop(0, tm // ROW_UNROLL, body, 0)

    @pl.when(i == 0)
    def _():
        gather(i, slot, True)

    @pl.when(i + 1 < pl.num_programs(0))
    def _():
        gather(i + 1, 1 - slot, True)

    gather(i, slot, False)
    w = w_ref[...]
    y1 = _unpack_halves(buf[slot, 0]).astype(F32)
    y2 = _unpack_halves(buf[slot, 1]).astype(F32)
    o_ref[...] = x_ref[...] + w[:, 0:1] * y1 + w[:, 1:2] * y2


def _combine(dest1, dest2, x, w12, ys):
    t, d = x.shape
    tm = ROW_TM
    dh = ys.shape[1]
    grid_spec = pltpu.PrefetchScalarGridSpec(
        num_scalar_prefetch=2,
        grid=(t // tm,),
        in_specs=[
            pl.BlockSpec((tm, d), lambda i, a, b: (i, 0)),
            pl.BlockSpec((tm, 2), lambda i, a, b: (i, 0)),
            pl.BlockSpec(memory_space=pl.ANY),
        ],
        out_specs=pl.BlockSpec((tm, d), lambda i, a, b: (i, 0)),
        scratch_shapes=[pltpu.VMEM((2, 2, tm, dh), I32), pltpu.SemaphoreType.DMA((2,))],
    )
    return pl.pallas_call(
        _combine_kernel,
        grid_spec=grid_spec,
        out_shape=jax.ShapeDtypeStruct((t, d), F32),
        compiler_params=_params("arbitrary"),
        name="moe_combine",
    )(dest1, dest2, x, w12, ys)


def _pad_cols(w, width):
    return jnp.pad(w, ((0, 0), (0, width - w.shape[1])))


def _moe_layer(x_mid, hn, logits, b_grp, b_exp, w_gate, w_up, w_down, layer):
    t, d = x_mid.shape
    bias = jnp.zeros((LANES,), F32)
    bias = bias.at[ROUTER_GROUP_ROW:ROUTER_GROUP_ROW + N_GROUPS].set(b_grp)
    bias = bias.at[ROUTER_EXPERT_ROW:ROUTER_EXPERT_ROW + N_EXPERTS].set(b_exp)
    routed, totals = _router(logits, jnp.broadcast_to(bias[:, None], (LANES, LANES)))
    p_rows = 2 * t + N_EXPERTS * MOE_BLOCK
    dest, tables = _plan(routed, totals, p_rows // MOE_BLOCK)
    dest1, dest2 = dest[0], dest[1]
    xs = _dispatch(dest1, dest2, tables, hn, p_rows)
    ys = _moe(tables, xs, w_gate, w_up, w_down, layer)
    w12 = jnp.stack([routed[4], routed[5]], axis=-1)
    return _combine(dest1, dest2, x_mid, w12, ys)


def _router_weight(w_grp, w_exp):
    d = w_grp.shape[0]
    wr = jnp.zeros((d, LANES), F32)
    wr = wr.at[:, ROUTER_GROUP_ROW:ROUTER_GROUP_ROW + N_GROUPS].set(w_grp)
    wr = wr.at[:, ROUTER_EXPERT_ROW:ROUTER_EXPERT_ROW + N_EXPERTS].set(w_exp)
    hi = wr.astype(BF16)
    lo = (wr - hi.astype(F32)).astype(BF16)
    return jnp.concatenate([hi, lo], axis=1)


def _fox_layer(x2, attn_g, w_stack, j, b_f, q_g, k_g, mem_kv, memq_g, memk_g, batch, seq, mem_len):
    qkvo = 4 * FOX_W
    w_in = w_stack[j]
    w_main = w_in[:, :qkvo].astype(BF16)
    w_side = jnp.concatenate([w_in[:, qkvo + FOX_HEADS:], w_in[:, qkvo:qkvo + FOX_HEADS]], axis=1)
    w_side = _pad_cols(w_side, MEM_W + LANES).astype(BF16)
    scale = LOG2E / math.sqrt(HEAD_DIM)
    head_gain = jnp.concatenate([jnp.tile(q_g * scale, FOX_HEADS), jnp.tile(k_g, FOX_HEADS),
                                 jnp.ones((qkvo - 2 * FOX_W,), F32)]).reshape(1, -1)
    proj, side = _norm_matmul(x2, attn_g, w_main, head_gain, w_side, n_norm_tiles=2 * FOX_W // FOX_PROJ_TN,
                              tm=PROJ_TM, name="fox_in_proj", tn=FOX_PROJ_TN)
    kb = _fox_decay(side, MEM_W // LANES, b_f, batch, seq)
    mix = _fox_attention(proj, kb, batch, seq)
    mo = _memory_attention(side, MEM_W, 0, 0, mem_kv, memq_g, memk_g, batch, seq, mem_len)
    return mix, mo


def _gla_layer(x2, attn_g, w_in, w_a2, b_a, o_g, mem_kv, memq_g, memk_g, batch, seq, mem_len):
    d = w_in.shape[0]
    kw = GLA_HEADS * GLA_DK
    vw = GLA_HEADS * GLA_DV
    pad = GLA_DKP - GLA_DK

    def pad_heads(w):
        lead = w.shape[0]
        return jnp.pad(w.reshape(lead, GLA_HEADS, GLA_DK), ((0, 0), (0, 0), (0, pad))).reshape(lead, -1)

    wq = pad_heads(w_in[:, :kw] * (1.0 / math.sqrt(GLA_DK)))
    wk = pad_heads(w_in[:, kw:2 * kw])
    rest = w_in[:, 2 * kw:2 * kw + 2 * vw]
    w_a1 = w_in[:, 2 * kw + 2 * vw:2 * kw + 2 * vw + GLA_RANK]
    w_qm = w_in[:, 2 * kw + 2 * vw + GLA_RANK:]
    w_main = jnp.concatenate([rest, wq, wk, w_qm], axis=1).astype(BF16)
    w_side = _pad_cols(w_a1, LANES).astype(BF16)
    head_gain = jnp.ones((1, w_main.shape[1]), F32)
    proj, a_side = _norm_matmul(x2, attn_g, w_main, head_gain, w_side, n_norm_tiles=0, tm=PROJ_TM,
                                name="gla_in_proj")
    wa = jnp.pad(pad_heads(w_a2), ((0, LANES - GLA_RANK), (0, 0)))
    ba = pad_heads(b_a.reshape(1, kw))
    mix = _gla(proj, a_side, wa, ba, o_g, batch, seq)
    q_col_block = (2 * GLA_HEADS * GLA_DKP + 2 * vw) // MEM_W
    mo = _memory_attention(proj, MEM_W, q_col_block, 0, mem_kv, memq_g, memk_g, batch, seq, mem_len)
    return mix, mo


def kernel(x, mem, mem_norm_g, w_mem_kv, attn_norm_g, fox_w_in, fox_b_f, fox_q_g, fox_k_g, gla_w_in, gla_w_a2,
           gla_b_a, gla_o_g, memq_g, memk_g, w_out, ffn_norm_g, w_grp, b_grp, w_exp, b_exp, w_gate, w_up, w_down):
    batch, seq, d = x.shape
    mem_len = mem.shape[1]
    depth = attn_norm_g.shape[0]
    x2 = x.reshape(batch * seq, d)

    n_kv = w_mem_kv.shape[1]
    mem_kv, _ = _norm_matmul(mem.reshape(batch * mem_len, d), mem_norm_g, w_mem_kv.astype(BF16),
                             jnp.ones((1, n_kv), F32), jnp.zeros((d, LANES), BF16), n_norm_tiles=0,
                             tm=batch * mem_len, name="mem_kv_proj")

    for i in range(depth):
        j = i // 2
        if i % 2 == 0:
            mix, mo = _fox_layer(x2, attn_norm_g[i], fox_w_in, j, fox_b_f[j], fox_q_g[j], fox_k_g[j],
                                 mem_kv, memq_g[i], memk_g[i], batch, seq, mem_len)
        else:
            mix, mo = _gla_layer(x2, attn_norm_g[i], gla_w_in[j], gla_w_a2[j], gla_b_a[j], gla_o_g[j],
                                 mem_kv, memq_g[i], memk_g[i], batch, seq, mem_len)
        w_o = w_out[i].astype(BF16)
        x_mid, hn, logits = _out_proj(mix, mo, w_o[:mix.shape[1]], w_o[mix.shape[1]:], x2, ffn_norm_g[i],
                                      _router_weight(w_grp[i], w_exp[i]))
        x2 = _moe_layer(x_mid, hn, logits, b_grp[i], b_exp[i], w_gate, w_up, w_down, i)
    return x2.reshape(batch, seq, d)
```

```python
import functools
import math

import jax
import jax.numpy as jnp
from jax import lax
from jax.experimental import pallas as pl
from jax.experimental.pallas import tpu as pltpu

F32 = jnp.float32
BF16 = jnp.bfloat16
I32 = jnp.int32

HEAD_DIM = 128
FOX_HEADS = 12
FOX_W = FOX_HEADS * HEAD_DIM
MEM_HEADS = 4
MEM_W = MEM_HEADS * HEAD_DIM
GLA_HEADS = 4
GLA_DV = FOX_W // GLA_HEADS
GLA_DK = GLA_DV // 2
GLA_DKP = 256
GLA_RANK = 16
GLA_TAU = 16.0
N_GROUPS = 4
EXPERTS_PER_GROUP = 8
N_EXPERTS = N_GROUPS * EXPERTS_PER_GROUP
RMS_EPS = 1e-6
LOG2E = 1.4426950408889634
GATE_GROUP = 16

LANES = 128
SUBLANES = 8
VMEM_LIMIT_BYTES = 56 * 1024 * 1024

PROJ_TM = 1024
PROJ_TN = 512
FOX_PROJ_TN = 1024
ATTN_T = 512
ATTN_HEADS_PER_STEP = 6
DECAY_TS = 512
MEM_TQ = 512
OUT_TM = 512
ROUTER_TT = 512
ROW_TM = 512
MOE_BLOCK = 256
ROW_UNROLL = 16
W_CHUNKS = 12
W_PHASE_CHUNKS = 3
W_CHUNK_SPLIT = 1
GLA_TS = 256
GLA_CHUNK = 64
ROUTER_GROUP_ROW = 0
ROUTER_EXPERT_ROW = 8
NEG_BIG = -1e30


def _params(*sem):
    return pltpu.CompilerParams(dimension_semantics=sem, vmem_limit_bytes=VMEM_LIMIT_BYTES)


def _log_sigmoid(z):
    return jnp.minimum(z, 0.0) - jnp.log1p(jnp.exp(-jnp.abs(z)))


def _dot(a, b):
    return jnp.dot(a, b, preferred_element_type=F32)


def _dot_nt(a, b):
    return lax.dot_general(a, b, (((1,), (1,)), ((), ())), preferred_element_type=F32)


def _dot_f32(a, b):
    return jnp.dot(a, b, preferred_element_type=F32, precision=lax.Precision.HIGHEST)


def _pack_halves(x):
    n = x.shape[1] // 2
    lo = lax.bitcast_convert_type(x[:, :n].astype(BF16).astype(F32), I32)
    hi = lax.bitcast_convert_type(x[:, n:].astype(BF16).astype(F32), I32)
    return lax.shift_right_logical(lo, 16) | hi


def _unpack_halves(w):
    lo = lax.bitcast_convert_type(lax.shift_left(w, 16), F32)
    hi = lax.bitcast_convert_type(w & jnp.int32(-65536), F32)
    return jnp.concatenate([lo.astype(BF16), hi.astype(BF16)], axis=1)


def _split3(x):
    hi = x.astype(BF16)
    r1 = x - hi.astype(F32)
    mid = r1.astype(BF16)
    lo = (r1 - mid.astype(F32)).astype(BF16)
    return hi, mid, lo


def _norm_matmul_kernel(x_ref, g_ref, w_ref, hg_ref, ws_ref, o_ref, os_ref, xn_ref, *, nj_main, n_norm_tiles):
    j = pl.program_id(1)

    @pl.when(j == 0)
    def _():
        x = x_ref[...]
        ms = jnp.mean(x * x, axis=-1, keepdims=True)
        xn_ref[...] = (x * lax.rsqrt(ms + RMS_EPS) * g_ref[...]).astype(BF16)

    if n_norm_tiles > 0:
        @pl.when(j < n_norm_tiles)
        def _():
            acc = _dot(xn_ref[...], w_ref[...])
            for h in range(acc.shape[1] // HEAD_DIM):
                sl = slice(h * HEAD_DIM, (h + 1) * HEAD_DIM)
                a = acc[:, sl]
                ms = jnp.mean(a * a, axis=-1, keepdims=True)
                o_ref[:, sl] = (a * lax.rsqrt(ms + RMS_EPS) * hg_ref[:, sl]).astype(o_ref.dtype)

    @pl.when((j >= n_norm_tiles) & (j < nj_main))
    def _():
        o_ref[...] = _dot(xn_ref[...], w_ref[...]).astype(o_ref.dtype)

    @pl.when(j == nj_main)
    def _():
        os_ref[...] = _dot(xn_ref[...], ws_ref[...])


def _norm_matmul(x, g, w_main, head_gain, w_side, *, n_norm_tiles, tm, name, tn=PROJ_TN):
    t, d = x.shape
    n_main = w_main.shape[1]
    nj_main = n_main // tn
    n_side = w_side.shape[1]
    assert t % tm == 0 and n_main % tn == 0 and n_side % LANES == 0
    last = nj_main - 1
    kernel = functools.partial(_norm_matmul_kernel, nj_main=nj_main, n_norm_tiles=n_norm_tiles)
    return pl.pallas_call(
        kernel,
        grid=(t // tm, nj_main + 1),
        in_specs=[
            pl.BlockSpec((tm, d), lambda i, j: (i, 0)),
            pl.BlockSpec((1, d), lambda i, j: (0, 0)),
            pl.BlockSpec((d, tn), lambda i, j: (0, jnp.minimum(j, last))),
            pl.BlockSpec((1, tn), lambda i, j: (0, jnp.minimum(j, last))),
            pl.BlockSpec((d, n_side), lambda i, j: (0, 0)),
        ],
        out_specs=[
            pl.BlockSpec((tm, tn), lambda i, j: (i, jnp.minimum(j, last))),
            pl.BlockSpec((tm, n_side), lambda i, j: (i, 0)),
        ],
        out_shape=[jax.ShapeDtypeStruct((t, n_main), BF16), jax.ShapeDtypeStruct((t, n_side), F32)],
        scratch_shapes=[pltpu.VMEM((tm, d), BF16)],
        compiler_params=_params("parallel", "arbitrary"),
        name=name,
    )(x, g.reshape(1, d), w_main, head_gain, w_side)


def _fox_decay_kernel(f_ref, b_ref, kb_ref, carry_ref):
    @pl.when(pl.program_id(1) == 0)
    def _():
        carry_ref[...] = jnp.zeros_like(carry_ref)

    lane = lax.broadcasted_iota(I32, f_ref.shape, 1)
    f = jnp.where(lane < FOX_HEADS, f_ref[...], 0.0)
    f = f + pltpu.roll(f, GATE_GROUP, 1) + pltpu.roll(f, 2 * GATE_GROUP, 1)
    lf = _log_sigmoid(f + b_ref[...]) * (-LOG2E)
    ts = lf.shape[0]
    r = lax.broadcasted_iota(I32, (ts, ts), 0)
    c = lax.broadcasted_iota(I32, (ts, ts), 1)
    incl = jnp.where(c <= r, 1.0, 0.0).astype(BF16)
    hi, mid, lo = _split3(lf)
    cs = _dot(incl, hi) + _dot(incl, mid) + _dot(incl, lo) + carry_ref[0:1, :]
    carry_ref[...] = jnp.broadcast_to(cs[ts - 1:ts, :], carry_ref.shape)
    hi, mid, lo = _split3(cs)
    piece = jnp.where(lane < GATE_GROUP, hi.astype(F32),
                      jnp.where(lane < 2 * GATE_GROUP, mid.astype(F32),
                                jnp.where(lane < 3 * GATE_GROUP, lo.astype(F32), 0.0)))
    kb_ref[...] = piece.astype(BF16)


def _fox_decay(f_side, col_block, b_f, batch, seq):
    ts = DECAY_TS
    ns = seq // ts
    b_pad = jnp.zeros((LANES,), F32)
    for p in range(3):
        b_pad = b_pad.at[p * GATE_GROUP:p * GATE_GROUP + FOX_HEADS].set(b_f)
    return pl.pallas_call(
        _fox_decay_kernel,
        grid=(batch, ns),
        in_specs=[
            pl.BlockSpec((ts, LANES), lambda b, s: (b * ns + s, col_block)),
            pl.BlockSpec((1, LANES), lambda b, s: (0, 0)),
        ],
        out_specs=pl.BlockSpec((ts, LANES), lambda b, s: (b * ns + s, 0)),
        out_shape=jax.ShapeDtypeStruct((batch * seq, LANES), BF16),
        scratch_shapes=[pltpu.VMEM((SUBLANES, LANES), F32)],
        compiler_params=_params("parallel", "arbitrary"),
        name="fox_decay",
    )(f_side, b_pad.reshape(1, LANES))


def _fox_attn_kernel(q_ref, k_ref, v_ref, kb_ref, og_ref, o_ref, m_ref, acc_ref):
    hp = pl.program_id(1)
    qi = pl.program_id(2)
    t = q_ref.shape[0]
    lane = lax.broadcasted_iota(I32, (t, LANES), 1)
    ones = jnp.ones((t, HEAD_DIM), BF16)
    q_aug = []
    for hh in range(ATTN_HEADS_PER_STEP):
        h = hp * ATTN_HEADS_PER_STEP + hh
        own = jnp.where(lane < 3 * GATE_GROUP, jnp.where((lane & (GATE_GROUP - 1)) == h, 1.0, 0.0), 0.0)
        q_aug.append(jnp.concatenate([q_ref[:, hh * HEAD_DIM:(hh + 1) * HEAD_DIM], own.astype(BF16)], axis=1))
    m_ref[...] = jnp.full_like(m_ref, NEG_BIG)
    acc_ref[...] = jnp.zeros_like(acc_ref)

    def tile(j, masked):
        rows = pl.ds(pl.multiple_of(j * t, t), t)
        kb = kb_ref[rows, :]
        for hh in range(ATTN_HEADS_PER_STEP):
            cols = slice(hh * HEAD_DIM, (hh + 1) * HEAD_DIM)
            k_aug = jnp.concatenate([k_ref[rows, cols], kb], axis=1)
            s = _dot_nt(q_aug[hh], k_aug)
            if masked:
                row = lax.broadcasted_iota(I32, (t, t), 0)
                col = lax.broadcasted_iota(I32, (t, t), 1)
                s = jnp.where(col <= row, s, NEG_BIG)
            m_prev = m_ref[hh]
            m_new = jnp.maximum(m_prev, jnp.max(s, axis=-1, keepdims=True))
            alpha = jnp.exp2(m_prev - m_new)
            p = jnp.exp2(s - jnp.concatenate([m_new] * (t // LANES), axis=1))
            v_aug = jnp.concatenate([v_ref[rows, cols], ones], axis=1)
            pv = _dot(p.astype(BF16), v_aug)
            acc_ref[hh] = jnp.concatenate([alpha, alpha], axis=1) * acc_ref[hh] + pv
            m_ref[hh] = m_new

    def full_tile(j, carry):
        tile(j, False)
        return carry

    lax.fori_loop(0, qi, full_tile, 0)
    tile(qi, True)
    for hh in range(ATTN_HEADS_PER_STEP):
        cols = slice(hh * HEAD_DIM, (hh + 1) * HEAD_DIM)
        acc = acc_ref[hh]
        og = og_ref[:, cols].astype(F32)
        o_ref[:, cols] = (acc[:, :HEAD_DIM] / acc[:, HEAD_DIM:] * jax.nn.sigmoid(og)).astype(o_ref.dtype)


def _fox_attention(proj, kb, batch, seq):
    t = ATTN_T
    nq = seq // t
    hps = ATTN_HEADS_PER_STEP
    w = hps * HEAD_DIM
    nhp = FOX_HEADS // hps
    return pl.pallas_call(
        _fox_attn_kernel,
        grid=(batch, nhp, nq),
        in_specs=[
            pl.BlockSpec((t, w), lambda b, h, i: (b * nq + i, h)),
            pl.BlockSpec((seq, w), lambda b, h, i: (b, nhp + h)),
            pl.BlockSpec((seq, w), lambda b, h, i: (b, 2 * nhp + h)),
            pl.BlockSpec((seq, LANES), lambda b, h, i: (b, 0)),
            pl.BlockSpec((t, w), lambda b, h, i: (b * nq + i, 3 * nhp + h)),
        ],
        out_specs=pl.BlockSpec((t, w), lambda b, h, i: (b * nq + i, h)),
        out_shape=jax.ShapeDtypeStruct((batch * seq, FOX_W), BF16),
        scratch_shapes=[pltpu.VMEM((hps, t, LANES), F32), pltpu.VMEM((hps, t, 2 * HEAD_DIM), F32)],
        compiler_params=_params("parallel", "parallel", "arbitrary"),
        name="fox_attention",
    )(proj, proj, proj, kb, proj)


def _mem_attn_kernel(q_ref, k_ref, v_ref, qg_ref, kg_ref, o_ref, *, q_offset):
    for h in range(MEM_HEADS):
        sl = slice(h * HEAD_DIM, (h + 1) * HEAD_DIM)
        q = q_ref[:, q_offset + h * HEAD_DIM:q_offset + (h + 1) * HEAD_DIM].astype(F32)
        q = q * lax.rsqrt(jnp.mean(q * q, axis=-1, keepdims=True) + RMS_EPS) * qg_ref[...]
        k = k_ref[:, sl].astype(F32)
        k = k * lax.rsqrt(jnp.mean(k * k, axis=-1, keepdims=True) + RMS_EPS) * kg_ref[...]
        s = _dot_nt(q.astype(BF16), k.astype(BF16))
        p = jnp.exp(s - jnp.max(s, axis=-1, keepdims=True))
        o = _dot(p.astype(BF16), v_ref[:, sl]) / jnp.sum(p, axis=-1, keepdims=True)
        o_ref[:, sl] = o.astype(o_ref.dtype)


def _memory_attention(q_src, q_block_w, q_col_block, q_offset, mem_kv, q_gain, k_gain, batch, seq, mem_len):
    tq = MEM_TQ
    nq = seq // tq
    scale = 1.0 / math.sqrt(HEAD_DIM)
    return pl.pallas_call(
        functools.partial(_mem_attn_kernel, q_offset=q_offset),
        grid=(batch * nq,),
        in_specs=[
            pl.BlockSpec((tq, q_block_w), lambda i: (i, q_col_block)),
            pl.BlockSpec((mem_len, MEM_W), lambda i: (i // nq, 0)),
            pl.BlockSpec((mem_len, MEM_W), lambda i: (i // nq, 1)),
            pl.BlockSpec((1, HEAD_DIM), lambda i: (0, 0)),
            pl.BlockSpec((1, HEAD_DIM), lambda i: (0, 0)),
        ],
        out_specs=pl.BlockSpec((tq, MEM_W), lambda i: (i, 0)),
        out_shape=jax.ShapeDtypeStruct((batch * seq, MEM_W), BF16),
        compiler_params=_params("parallel"),
        name="memory_attention",
    )(q_src, mem_kv, mem_kv, (q_gain * scale).reshape(1, HEAD_DIM), k_gain.reshape(1, HEAD_DIM))


def _gla_kernel(q_ref, k_ref, v_ref, r_ref, a_ref, wa_ref, ba_ref, og_ref, o_ref,
                st_ref, qe_ref, q2_ref, k2_ref, kd_ref, oi_ref):
    @pl.when(pl.program_id(1) == 0)
    def _():
        st_ref[...] = jnp.zeros_like(st_ref)

    ts = q_ref.shape[0]
    c = GLA_CHUNK
    shift = c.bit_length() - 1
    n_chunks = ts // c

    z = _dot_f32(a_ref[...], wa_ref[...]) + ba_ref[...]
    g = _log_sigmoid(z) * (1.0 / GLA_TAU)
    ri = lax.broadcasted_iota(I32, (ts, ts), 0)
    ci = lax.broadcasted_iota(I32, (ts, ts), 1)
    chunk_start = lax.shift_left(lax.shift_right_logical(ri, shift), shift)

    def chunk_causal(val):
        return jnp.where(ci <= ri, jnp.where(ci >= chunk_start, val, 0.0), 0.0)

    incl = chunk_causal(jnp.ones((ts, ts), F32)).astype(BF16)
    hi, mid, lo = _split3(g)
    bc = _dot(incl, hi) + _dot(incl, mid) + _dot(incl, lo)
    decay = []
    for ic in range(n_chunks):
        rows = slice(ic * c, (ic + 1) * c)
        bcc = bc[rows]
        b_mid = bcc[c // 2 - 1:c // 2]
        b_last = bcc[c - 1:c]
        q = q_ref[rows, :].astype(F32)
        k = k_ref[rows, :].astype(F32)
        qe_ref[rows, :] = (q * jnp.exp(bcc)).astype(BF16)
        q2_ref[rows, :] = (q * jnp.exp(bcc - b_mid)).astype(BF16)
        k2_ref[rows, :] = (k * jnp.exp(b_mid - bcc)).astype(BF16)
        kd_ref[rows, :] = (k * jnp.exp(b_last - bcc)).astype(BF16)
        decay.append(jnp.exp(b_last))
    for h in range(GLA_HEADS):
        ks = slice(h * GLA_DKP, (h + 1) * GLA_DKP)
        vs = slice(h * GLA_DV, (h + 1) * GLA_DV)
        att = chunk_causal(_dot_nt(q2_ref[:, ks], k2_ref[:, ks]))
        oi_ref[:, vs] = _dot(att.astype(BF16), v_ref[:, vs])

    for ic in range(n_chunks):
        rows = slice(ic * c, (ic + 1) * c)
        for h in range(GLA_HEADS):
            ks = slice(h * GLA_DKP, (h + 1) * GLA_DKP)
            vs = slice(h * GLA_DV, (h + 1) * GLA_DV)
            st = st_ref[h]
            o = _dot_nt(qe_ref[rows, ks], st.astype(BF16)) + oi_ref[rows, vs]
            upd = lax.dot_general(v_ref[rows, vs], kd_ref[rows, ks], (((0,), (0,)), ((), ())),
                                  preferred_element_type=F32)
            st_ref[h] = st * decay[ic][:, ks] + upd
            on = o * lax.rsqrt(jnp.mean(o * o, axis=-1, keepdims=True) + RMS_EPS) * og_ref[...]
            rh = r_ref[rows, vs].astype(F32)
            o_ref[rows, vs] = (on * rh * jax.nn.sigmoid(rh)).astype(o_ref.dtype)


def _gla(proj, a_side, wa, ba, o_gain, batch, seq):
    ts = GLA_TS
    ns = seq // ts
    kw = GLA_HEADS * GLA_DKP
    vw = GLA_HEADS * GLA_DV
    assert (2 * vw) % kw == 0
    q_blk = (2 * vw) // kw
    return pl.pallas_call(
        _gla_kernel,
        grid=(batch, ns),
        in_specs=[
            pl.BlockSpec((ts, kw), lambda b, s: (b * ns + s, q_blk)),
            pl.BlockSpec((ts, kw), lambda b, s: (b * ns + s, q_blk + 1)),
            pl.BlockSpec((ts, vw), lambda b, s: (b * ns + s, 0)),
            pl.BlockSpec((ts, vw), lambda b, s: (b * ns + s, 1)),
            pl.BlockSpec((ts, LANES), lambda b, s: (b * ns + s, 0)),
            pl.BlockSpec((LANES, kw), lambda b, s: (0, 0)),
            pl.BlockSpec((1, kw), lambda b, s: (0, 0)),
            pl.BlockSpec((1, GLA_DV), lambda b, s: (0, 0)),
        ],
        out_specs=pl.BlockSpec((ts, vw), lambda b, s: (b * ns + s, 0)),
        out_shape=jax.ShapeDtypeStruct((batch * seq, vw), BF16),
        scratch_shapes=[pltpu.VMEM((GLA_HEADS, GLA_DV, GLA_DKP), F32)] + [pltpu.VMEM((ts, kw), BF16)] * 4
        + [pltpu.VMEM((ts, vw), F32)],
        compiler_params=_params("parallel", "arbitrary"),
        name="gla",
    )(proj, proj, proj, proj, a_side, wa, ba, o_gain.reshape(1, GLA_DV))


def _out_proj_kernel(mix_ref, mo_ref, w1_ref, w2_ref, x_ref, g_ref, wr_ref, xo_ref, hn_ref, lg_ref):
    y = _dot(mix_ref[...], w1_ref[...]) + _dot(mo_ref[...], w2_ref[...])
    xn = x_ref[...] + y
    xo_ref[...] = xn
    hn = xn * lax.rsqrt(jnp.mean(xn * xn, axis=-1, keepdims=True) + RMS_EPS) * g_ref[...]
    hn_ref[...] = _pack_halves(hn)
    h_hi = hn.astype(BF16)
    h_lo = (hn - h_hi.astype(F32)).astype(BF16)
    both = _dot(h_hi, wr_ref[...])
    lg_ref[...] = both[:, :LANES] + both[:, LANES:] + _dot(h_lo, wr_ref[:, :LANES])


def _out_proj(mix, mo, w1, w2, x, g, wr):
    t, d = x.shape
    tm = OUT_TM
    once = pl.Buffered(1)
    return pl.pallas_call(
        _out_proj_kernel,
        grid=(t // tm,),
        in_specs=[
            pl.BlockSpec((tm, mix.shape[1]), lambda i: (i, 0)),
            pl.BlockSpec((tm, mo.shape[1]), lambda i: (i, 0)),
            pl.BlockSpec(w1.shape, lambda i: (0, 0), pipeline_mode=once),
            pl.BlockSpec(w2.shape, lambda i: (0, 0), pipeline_mode=once),
            pl.BlockSpec((tm, d), lambda i: (i, 0)),
            pl.BlockSpec((1, d), lambda i: (0, 0), pipeline_mode=once),
            pl.BlockSpec((d, 2 * LANES), lambda i: (0, 0), pipeline_mode=once),
        ],
        out_specs=[
            pl.BlockSpec((tm, d), lambda i: (i, 0)),
            pl.BlockSpec((tm, d // 2), lambda i: (i, 0)),
            pl.BlockSpec((tm, LANES), lambda i: (i, 0)),
        ],
        out_shape=[jax.ShapeDtypeStruct((t, d), F32), jax.ShapeDtypeStruct((t, d // 2), I32),
                   jax.ShapeDtypeStruct((t, LANES), F32)],
        compiler_params=_params("parallel"),
        name="out_proj",
    )(mix, mo, w1, w2, x, g.reshape(1, d), wr)


def _router_kernel(lg_ref, b_ref, o_ref, cnt_ref, carry_ref):
    @pl.when(pl.program_id(0) == 0)
    def _():
        carry_ref[...] = jnp.zeros_like(carry_ref)

    logits = lg_ref[...].T + b_ref[:, 0:1]
    tt = logits.shape[1]
    gl = [logits[ROUTER_GROUP_ROW + g:ROUTER_GROUP_ROW + g + 1, :] for g in range(N_GROUPS)]
    best = gl[0]
    gidx = jnp.zeros((1, tt), I32)
    for g in range(1, N_GROUPS):
        better = gl[g] > best
        best = jnp.where(better, gl[g], best)
        gidx = jnp.where(better, g, gidx)
    denom = jnp.exp(gl[0] - best)
    for g in range(1, N_GROUPS):
        denom = denom + jnp.exp(gl[g] - best)
    g_p = 1.0 / denom

    epg = EXPERTS_PER_GROUP
    in_grp = logits[ROUTER_EXPERT_ROW:ROUTER_EXPERT_ROW + epg, :]
    for g in range(1, N_GROUPS):
        lo = ROUTER_EXPERT_ROW + g * epg
        in_grp = jnp.where(gidx == g, logits[lo:lo + epg, :], in_grp)
    rows = lax.broadcasted_iota(I32, (epg, tt), 0).astype(F32)
    m1 = jnp.max(in_grp, axis=0, keepdims=True)
    i1 = jnp.min(jnp.where(in_grp == m1, rows, float(epg)), axis=0, keepdims=True)
    rest = jnp.where(rows == i1, -jnp.inf, in_grp)
    m2 = jnp.max(rest, axis=0, keepdims=True)
    i2 = jnp.min(jnp.where(rest == m2, rows, float(epg)), axis=0, keepdims=True)
    e21 = jnp.exp(m2 - m1)
    w1 = g_p / (1.0 + e21)
    w2 = g_p * e21 / (1.0 + e21)
    e1 = gidx * epg + i1.astype(I32)
    e2 = gidx * epg + i2.astype(I32)

    er = lax.broadcasted_iota(I32, (N_EXPERTS, tt), 0)
    oh1 = er == e1
    oh2 = er == e2
    cnt = jnp.where(oh1, 1.0, 0.0) + jnp.where(oh2, 1.0, 0.0)
    r = lax.broadcasted_iota(I32, (tt, tt), 0)
    c = lax.broadcasted_iota(I32, (tt, tt), 1)
    strict = jnp.where(r < c, 1.0, 0.0).astype(BF16)
    before = _dot(cnt.astype(BF16), strict) + carry_ref[:, 0:1]
    rank1 = jnp.sum(jnp.where(oh1, before, 0.0), axis=0, keepdims=True)
    rank2 = jnp.sum(jnp.where(oh2, before, 0.0), axis=0, keepdims=True)
    total = carry_ref[...] + jnp.sum(cnt, axis=1, keepdims=True)
    carry_ref[...] = total
    cnt_ref[...] = total

    o_ref[0:1, :] = e1.astype(F32)
    o_ref[1:2, :] = e2.astype(F32)
    o_ref[2:3, :] = rank1
    o_ref[3:4, :] = rank2
    o_ref[4:5, :] = w1
    o_ref[5:6, :] = w2
    o_ref[6:8, :] = jnp.zeros((2, tt), F32)


def _router(logits, bias_col):
    t = logits.shape[0]
    tt = ROUTER_TT
    return pl.pallas_call(
        _router_kernel,
        grid=(t // tt,),
        in_specs=[
            pl.BlockSpec((tt, LANES), lambda i: (i, 0)),
            pl.BlockSpec((LANES, LANES), lambda i: (0, 0)),
        ],
        out_specs=[
            pl.BlockSpec((SUBLANES, tt), lambda i: (0, i)),
            pl.BlockSpec((N_EXPERTS, LANES), lambda i: (0, 0)),
        ],
        out_shape=[jax.ShapeDtypeStruct((SUBLANES, t), F32), jax.ShapeDtypeStruct((N_EXPERTS, LANES), F32)],
        scratch_shapes=[pltpu.VMEM((N_EXPERTS, LANES), F32)],
        compiler_params=_params("arbitrary"),
        name="router",
    )(logits, bias_col)


TAB_EXPERT, TAB_VALID, TAB_FIRST, TAB_ORD, TAB_SLOT, TAB_NEXT, TAB_DONE, TAB_ROWS = range(8)


def _plan_kernel(r_ref, tot_ref, dest_ref, tab_ref):
    ne = N_EXPERTS
    blk = float(MOE_BLOCK)
    counts = tot_ref[...]
    nblk = jnp.floor((counts + (blk - 1.0)) * (1.0 / blk))
    padded = nblk * blk
    er = lax.broadcasted_iota(I32, (ne, ne), 0)
    ec = lax.broadcasted_iota(I32, (ne, ne), 1)
    tri = jnp.where(ec <= er, 1.0, 0.0)
    pad_ends = _dot_f32(tri, padded)
    pad_starts = pad_ends - padded
    ps_col = pad_starts[:, 0:1]

    tt = r_ref.shape[1]
    eid = lax.broadcasted_iota(I32, (ne, tt), 0).astype(F32)
    for k in range(2):
        start = jnp.sum(jnp.where(eid == r_ref[k:k + 1, :], ps_col, 0.0), axis=0, keepdims=True)
        dest_ref[k:k + 1, :] = (start + r_ref[k + 2:k + 3, :]).astype(I32)

    @pl.when(pl.program_id(0) == 0)
    def _():
        n_phases = float(W_CHUNKS // W_PHASE_CHUNKS)
        nonempty = jnp.where(nblk > 0.0, 1.0, 0.0)
        rank = _dot_f32(tri, nonempty) - 1.0
        slot_e = rank - 2.0 * jnp.floor(rank * 0.5)
        lane0 = jnp.where(lax.broadcasted_iota(I32, (SUBLANES, LANES), 1) == 0, 1.0, 0.0)
        ne_row = _dot_nt(lane0, nonempty)[0:1, :]
        nb_row = _dot_nt(lane0, nblk)[0:1, :]
        ecf = ec.astype(F32)
        cand = jnp.where(ec > er, jnp.where(ne_row > 0.0, ecf, float(ne)), float(ne))
        nxt_e = jnp.min(cand, axis=1, keepdims=True)
        nxt_e = jnp.where(nxt_e == float(ne), -1.0, nxt_e)
        cand = jnp.where(ec < er, jnp.where(ne_row > 0.0, ecf, -1.0), -1.0)
        prev_e = jnp.max(cand, axis=1, keepdims=True)
        nb_prev = jnp.sum(jnp.where(ecf == prev_e, nb_row, 0.0), axis=1, keepdims=True)
        done_e = jnp.where(prev_e >= 0.0, jnp.minimum(nb_prev, n_phases), 0.0)

        nbl = tab_ref.shape[1]
        brow = lax.broadcasted_iota(I32, (1, nbl), 1).astype(F32) * blk
        be = jnp.sum(jnp.where(brow >= pad_ends[:, 0:1], 1.0, 0.0), axis=0, keepdims=True)
        be = jnp.minimum(be, float(ne - 1))
        valid = jnp.where(brow < pad_ends[ne - 1:ne, 0:1], 1.0, 0.0)
        own = lax.broadcasted_iota(I32, (ne, nbl), 0).astype(F32) == be

        def pick(col):
            return jnp.sum(jnp.where(own, col, 0.0), axis=0, keepdims=True)

        ordinal = (brow - pick(ps_col)) * (1.0 / blk)
        first = valid * jnp.where(ordinal == 0.0, 1.0, 0.0)
        used = jnp.clip(pick(counts[:, 0:1]) - ordinal * blk, 0.0, blk)
        rows = {TAB_EXPERT: be, TAB_VALID: valid, TAB_FIRST: first, TAB_ORD: ordinal, TAB_SLOT: pick(slot_e[:, 0:1]),
                TAB_NEXT: pick(nxt_e), TAB_DONE: pick(done_e), TAB_ROWS: used}
        assert len(rows) == tab_ref.shape[0]
        for i, row in rows.items():
            tab_ref[i:i + 1, :] = row.astype(I32)


def _plan(routed, totals, n_blocks):
    t = routed.shape[1]
    tt = ROUTER_TT
    nbl = -(-n_blocks // LANES) * LANES
    return pl.pallas_call(
        _plan_kernel,
        grid=(t // tt,),
        in_specs=[
            pl.BlockSpec((SUBLANES, tt), lambda i: (0, i)),
            pl.BlockSpec((N_EXPERTS, LANES), lambda i: (0, 0)),
        ],
        out_specs=[
            pl.BlockSpec((2, tt), lambda i: (0, i)),
            pl.BlockSpec((SUBLANES, nbl), lambda i: (0, 0)),
        ],
        out_shape=[jax.ShapeDtypeStruct((2, t), I32), jax.ShapeDtypeStruct((SUBLANES, nbl), I32)],
        compiler_params=_params("arbitrary"),
        name="moe_plan",
    )(routed, totals)


def _row_copy(src_ref, src_row, dst_ref, dst_row, sem):
    return pltpu.make_async_copy(src_ref.at[pl.ds(src_row, 1), :], dst_ref.at[pl.ds(dst_row, 1), :], sem)


def _dispatch_kernel(d1_ref, d2_ref, tab_ref, h_ref, xs_ref, zero_ref, sem, zsem):
    tm = h_ref.shape[0]
    base = pl.program_id(0) * tm

    @pl.when(pl.program_id(0) == 0)
    def _():
        zero_ref[...] = jnp.zeros_like(zero_ref)

        def per_block(act):
            def body(b, carry):
                block_rows = xs_ref.at[pl.ds(pl.multiple_of(b * MOE_BLOCK, MOE_BLOCK), MOE_BLOCK), :]

                @pl.when(tab_ref[TAB_VALID, b] == 0)
                def _():
                    act(pltpu.make_async_copy(zero_ref, block_rows, zsem))

                @pl.when(tab_ref[TAB_VALID, b] != 0)
                def _():
                    used = tab_ref[TAB_ROWS, b]
                    head = (-used) & (SUBLANES - 1)
                    for r in range(SUBLANES - 1):
                        @pl.when(r < head)
                        def _():
                            act(_row_copy(zero_ref, 0, xs_ref, b * MOE_BLOCK + used + r, zsem))

                    rest = MOE_BLOCK - used - head
                    pos = b * MOE_BLOCK + used + head
                    for bit in range(SUBLANES.bit_length() - 1, MOE_BLOCK.bit_length() - 1):
                        size = 1 << bit

                        @pl.when((rest & size) != 0)
                        def _():
                            act(pltpu.make_async_copy(zero_ref.at[pl.ds(0, size), :],
                                                      xs_ref.at[pl.ds(pl.multiple_of(pos, SUBLANES), size), :], zsem))

                        pos = pos + (rest & size)

                return carry

            lax.fori_loop(0, xs_ref.shape[0] // MOE_BLOCK, body, 0)

        per_block(lambda cp: cp.start())
        per_block(lambda cp: cp.wait())

    def issue(g, carry):
        for k in range(ROW_UNROLL):
            r = g * ROW_UNROLL + k
            _row_copy(h_ref, r, xs_ref, d1_ref[base + r], sem).start()
            _row_copy(h_ref, r, xs_ref, d2_ref[base + r], sem).start()
        return carry

    lax.fori_loop(0, tm // ROW_UNROLL, issue, 0)

    def drain(g, carry):
        for k in range(ROW_UNROLL):
            r = g * ROW_UNROLL + k
            _row_copy(h_ref, r, xs_ref, 0, sem).wait()
            _row_copy(h_ref, r, xs_ref, 0, sem).wait()
        return carry

    lax.fori_loop(0, tm // ROW_UNROLL, drain, 0)


def _dispatch(dest1, dest2, tables, hn, p_rows):
    t, d = hn.shape
    tm = ROW_TM
    grid_spec = pltpu.PrefetchScalarGridSpec(
        num_scalar_prefetch=3,
        grid=(t // tm,),
        in_specs=[pl.BlockSpec((tm, d), lambda i, *_: (i, 0))],
        out_specs=pl.BlockSpec(memory_space=pl.ANY),
        scratch_shapes=[pltpu.VMEM((MOE_BLOCK, d), hn.dtype), pltpu.SemaphoreType.DMA, pltpu.SemaphoreType.DMA],
    )
    return pl.pallas_call(
        _dispatch_kernel,
        grid_spec=grid_spec,
        out_shape=jax.ShapeDtypeStruct((p_rows, d), hn.dtype),
        compiler_params=_params("arbitrary"),
        name="moe_dispatch",
    )(dest1, dest2, tables, hn)


def _weight_chunk(c, d, de):
    per = W_CHUNKS // 3
    if c < 2 * per:
        return c // per, (c % per) * (d // per), 0
    r, h = divmod(c - 2 * per, d // de)
    return 2, r * (d // per), h * de


def _moe_kernel(tab_ref, x_ref, wg_hbm, wu_hbm, wd_hbm, y_ref, wg_c, wu_c, wd_c, stage_ref, sems, *, layer):
    b = pl.program_id(0)
    expert = tab_ref[TAB_EXPERT, b]
    valid = tab_ref[TAB_VALID, b]
    first = tab_ref[TAB_FIRST, b]
    ordinal = tab_ref[TAB_ORD, b]
    slot = tab_ref[TAB_SLOT, b]
    nxt = tab_ref[TAB_NEXT, b]
    done = tab_ref[TAB_DONE, b]
    hbm = (wg_hbm, wu_hbm, wd_hbm)
    cache = (wg_c, wu_c, wd_c)
    d, de = wg_c.shape[1], wg_c.shape[2]
    rows, cols = stage_ref.shape[1], stage_ref.shape[2]

    sub = rows // W_CHUNK_SPLIT

    def chunk_copies(e, c, k):
        ti, r0, c0 = _weight_chunk(c, d, de)
        return [pltpu.make_async_copy(hbm[ti].at[layer, e, pl.ds(r0 + i * sub, sub), pl.ds(c0, cols)],
                                      stage_ref.at[k, pl.ds(i * sub, sub), :], sems.at[k])
                for i in range(W_CHUNK_SPLIT)]

    def start_phase(e, p):
        for k in range(W_PHASE_CHUNKS):
            for cp in chunk_copies(e, p * W_PHASE_CHUNKS + k, k):
                cp.start()

    def finish_phase(e, s, p):
        for k in range(W_PHASE_CHUNKS):
            c = p * W_PHASE_CHUNKS + k
            for cp in chunk_copies(e, c, k):
                cp.wait()
            ti, r0, c0 = _weight_chunk(c, d, de)
            cache[ti][s, pl.ds(r0, rows), pl.ds(c0, cols)] = stage_ref[k].astype(BF16)

    n_phases = W_CHUNKS // W_PHASE_CHUNKS

    @pl.when(valid != 0)
    def _():
        prev = jnp.maximum(b - 1, 0)
        prev_ord = tab_ref[TAB_ORD, prev]
        prev_nxt = tab_ref[TAB_NEXT, prev]
        pending = (b > 0) & (prev_ord < n_phases) & (prev_nxt >= 0)
        for p in range(n_phases):
            @pl.when(pending & (prev_ord == p))
            def _():
                finish_phase(prev_nxt, 1 - tab_ref[TAB_SLOT, prev], p)

        @pl.when(first != 0)
        def _():
            for p in range(n_phases):
                @pl.when(done <= p)
                def _():
                    start_phase(expert, p)
                    finish_phase(expert, slot, p)

        prefetch = (ordinal < n_phases) & (nxt >= 0)
        for p in range(n_phases):
            @pl.when(prefetch & (ordinal == p))
            def _():
                start_phase(nxt, p)

        x = _unpack_halves(x_ref[...])
        g = _dot(x, wg_c[slot])
        u = _dot(x, wu_c[slot])
        hid = (g * jax.nn.sigmoid(g) * u).astype(BF16)
        y_ref[...] = _pack_halves(_dot(hid, wd_c[slot]))

    @pl.when(valid == 0)
    def _():
        y_ref[...] = jnp.zeros_like(y_ref)


def _moe(tables, xs, wg, wu, wd, layer):
    p_rows, dh = xs.shape
    tm = MOE_BLOCK
    d, de = wg.shape[2], wg.shape[3]
    assert d == 2 * dh and wd.shape[2:] == (de, d) and d % (W_CHUNKS // 3) == 0 and d % de == 0
    chunk = (d // (W_CHUNKS // 3), de)
    grid_spec = pltpu.PrefetchScalarGridSpec(
        num_scalar_prefetch=1,
        grid=(p_rows // tm,),
        in_specs=[
            pl.BlockSpec((tm, dh), lambda i, *_: (i, 0)),
            pl.BlockSpec(memory_space=pl.ANY),
            pl.BlockSpec(memory_space=pl.ANY),
            pl.BlockSpec(memory_space=pl.ANY),
        ],
        out_specs=pl.BlockSpec((tm, dh), lambda i, *_: (i, 0)),
        scratch_shapes=[
            pltpu.VMEM((2, d, de), BF16), pltpu.VMEM((2, d, de), BF16), pltpu.VMEM((2, de, d), BF16),
            pltpu.VMEM((W_PHASE_CHUNKS,) + chunk, F32), pltpu.SemaphoreType.DMA((W_PHASE_CHUNKS,)),
        ],
    )
    return pl.pallas_call(
        functools.partial(_moe_kernel, layer=layer),
        grid_spec=grid_spec,
        out_shape=jax.ShapeDtypeStruct((p_rows, dh), I32),
        compiler_params=_params("arbitrary"),
        name="moe_experts",
    )(tables, xs, wg, wu, wd)


def _combine_kernel(d1_ref, d2_ref, x_ref, w_ref, ys_ref, o_ref, buf, sems):
    tm = x_ref.shape[0]
    i = pl.program_id(0)
    slot = i % 2

    def gather(step, s, start):
        base = step * tm

        def body(g, carry):
            for k in range(ROW_UNROLL):
                r = g * ROW_UNROLL + k
                if start:
                    _row_copy(ys_ref, d1_ref[base + r], buf.at[s, 0], r, sems.at[s]).start()
                    _row_copy(ys_ref, d2_ref[base + r], buf.at[s, 1], r, sems.at[s]).start()
                else:
                    _row_copy(ys_ref, 0, buf.at[s, 0], r, sems.at[s]).wait()
                    _row_copy(ys_ref, 0, buf.at[s, 1], r, sems.at[s]).wait()
            return carry

        lax.fori_loop(0, tm // ROW_UNROLL, body, 0)

    @pl.when(i == 0)
    def _():
        gather(i, slot, True)

    @pl.when(i + 1 < pl.num_programs(0))
    def _():
        gather(i + 1, 1 - slot, True)

    gather(i, slot, False)
    w = w_ref[...]
    y1 = _unpack_halves(buf[slot, 0]).astype(F32)
    y2 = _unpack_halves(buf[slot, 1]).astype(F32)
    o_ref[...] = x_ref[...] + w[:, 0:1] * y1 + w[:, 1:2] * y2


def _combine(dest1, dest2, x, w12, ys):
    t, d = x.shape
    tm = ROW_TM
    dh = ys.shape[1]
    grid_spec = pltpu.PrefetchScalarGridSpec(
        num_scalar_prefetch=2,
        grid=(t // tm,),
        in_specs=[
            pl.BlockSpec((tm, d), lambda i, a, b: (i, 0)),
            pl.BlockSpec((tm, 2), lambda i, a, b: (i, 0)),
            pl.BlockSpec(memory_space=pl.ANY),
        ],
        out_specs=pl.BlockSpec((tm, d), lambda i, a, b: (i, 0)),
        scratch_shapes=[pltpu.VMEM((2, 2, tm, dh), I32), pltpu.SemaphoreType.DMA((2,))],
    )
    return pl.pallas_call(
        _combine_kernel,
        grid_spec=grid_spec,
        out_shape=jax.ShapeDtypeStruct((t, d), F32),
        compiler_params=_params("arbitrary"),
        name="moe_combine",
    )(dest1, dest2, x, w12, ys)


def _pad_cols(w, width):
    return jnp.pad(w, ((0, 0), (0, width - w.shape[1])))


def _moe_layer(x_mid, hn, logits, b_grp, b_exp, w_gate, w_up, w_down, layer):
    t, d = x_mid.shape
    bias = jnp.zeros((LANES,), F32)
    bias = bias.at[ROUTER_GROUP_ROW:ROUTER_GROUP_ROW + N_GROUPS].set(b_grp)
    bias = bias.at[ROUTER_EXPERT_ROW:ROUTER_EXPERT_ROW + N_EXPERTS].set(b_exp)
    routed, totals = _router(logits, jnp.broadcast_to(bias[:, None], (LANES, LANES)))
    p_rows = 2 * t + N_EXPERTS * MOE_BLOCK
    dest, tables = _plan(routed, totals, p_rows // MOE_BLOCK)
    dest1, dest2 = dest[0], dest[1]
    xs = _dispatch(dest1, dest2, tables, hn, p_rows)
    ys = _moe(tables, xs, w_gate, w_up, w_down, layer)
    w12 = jnp.stack([routed[4], routed[5]], axis=-1)
    return _combine(dest1, dest2, x_mid, w12, ys)


def _router_weight(w_grp, w_exp):
    d = w_grp.shape[0]
    wr = jnp.zeros((d, LANES), F32)
    wr = wr.at[:, ROUTER_GROUP_ROW:ROUTER_GROUP_ROW + N_GROUPS].set(w_grp)
    wr = wr.at[:, ROUTER_EXPERT_ROW:ROUTER_EXPERT_ROW + N_EXPERTS].set(w_exp)
    hi = wr.astype(BF16)
    lo = (wr - hi.astype(F32)).astype(BF16)
    return jnp.concatenate([hi, lo], axis=1)


def _fox_layer(x2, attn_g, w_stack, j, b_f, q_g, k_g, mem_kv, memq_g, memk_g, batch, seq, mem_len):
    qkvo = 4 * FOX_W
    w_in = w_stack[j]
    w_main = w_in[:, :qkvo].astype(BF16)
    w_side = jnp.concatenate([w_in[:, qkvo + FOX_HEADS:], w_in[:, qkvo:qkvo + FOX_HEADS]], axis=1)
    w_side = _pad_cols(w_side, MEM_W + LANES).astype(BF16)
    scale = LOG2E / math.sqrt(HEAD_DIM)
    head_gain = jnp.concatenate([jnp.tile(q_g * scale, FOX_HEADS), jnp.tile(k_g, FOX_HEADS),
                                 jnp.ones((qkvo - 2 * FOX_W,), F32)]).reshape(1, -1)
    proj, side = _norm_matmul(x2, attn_g, w_main, head_gain, w_side, n_norm_tiles=2 * FOX_W // FOX_PROJ_TN,
                              tm=PROJ_TM, name="fox_in_proj", tn=FOX_PROJ_TN)
    kb = _fox_decay(side, MEM_W // LANES, b_f, batch, seq)
    mix = _fox_attention(proj, kb, batch, seq)
    mo = _memory_attention(side, MEM_W, 0, 0, mem_kv, memq_g, memk_g, batch, seq, mem_len)
    return mix, mo


def _gla_layer(x2, attn_g, w_in, w_a2, b_a, o_g, mem_kv, memq_g, memk_g, batch, seq, mem_len):
    d = w_in.shape[0]
    kw = GLA_HEADS * GLA_DK
    vw = GLA_HEADS * GLA_DV
    pad = GLA_DKP - GLA_DK

    def pad_heads(w):
        lead = w.shape[0]
        return jnp.pad(w.reshape(lead, GLA_HEADS, GLA_DK), ((0, 0), (0, 0), (0, pad))).reshape(lead, -1)

    wq = pad_heads(w_in[:, :kw] * (1.0 / math.sqrt(GLA_DK)))
    wk = pad_heads(w_in[:, kw:2 * kw])
    rest = w_in[:, 2 * kw:2 * kw + 2 * vw]
    w_a1 = w_in[:, 2 * kw + 2 * vw:2 * kw + 2 * vw + GLA_RANK]
    w_qm = w_in[:, 2 * kw + 2 * vw + GLA_RANK:]
    w_main = jnp.concatenate([rest, wq, wk, w_qm], axis=1).astype(BF16)
    w_side = _pad_cols(w_a1, LANES).astype(BF16)
    head_gain = jnp.ones((1, w_main.shape[1]), F32)
    proj, a_side = _norm_matmul(x2, attn_g, w_main, head_gain, w_side, n_norm_tiles=0, tm=PROJ_TM,
                                name="gla_in_proj")
    wa = jnp.pad(pad_heads(w_a2), ((0, LANES - GLA_RANK), (0, 0)))
    ba = pad_heads(b_a.reshape(1, kw))
    mix = _gla(proj, a_side, wa, ba, o_g, batch, seq)
    q_col_block = (2 * GLA_HEADS * GLA_DKP + 2 * vw) // MEM_W
    mo = _memory_attention(proj, MEM_W, q_col_block, 0, mem_kv, memq_g, memk_g, batch, seq, mem_len)
    return mix, mo


def kernel(x, mem, mem_norm_g, w_mem_kv, attn_norm_g, fox_w_in, fox_b_f, fox_q_g, fox_k_g, gla_w_in, gla_w_a2,
           gla_b_a, gla_o_g, memq_g, memk_g, w_out, ffn_norm_g, w_grp, b_grp, w_exp, b_exp, w_gate, w_up, w_down):
    batch, seq, d = x.shape
    mem_len = mem.shape[1]
    depth = attn_norm_g.shape[0]
    x2 = x.reshape(batch * seq, d)

    n_kv = w_mem_kv.shape[1]
    mem_kv, _ = _norm_matmul(mem.reshape(batch * mem_len, d), mem_norm_g, w_mem_kv.astype(BF16),
                             jnp.ones((1, n_kv), F32), jnp.zeros((d, LANES), BF16), n_norm_tiles=0,
                             tm=batch * mem_len, name="mem_kv_proj")

    for i in range(depth):
        j = i // 2
        if i % 2 == 0:
            mix, mo = _fox_layer(x2, attn_norm_g[i], fox_w_in, j, fox_b_f[j], fox_q_g[j], fox_k_g[j],
                                 mem_kv, memq_g[i], memk_g[i], batch, seq, mem_len)
        else:
            mix, mo = _gla_layer(x2, attn_norm_g[i], gla_w_in[j], gla_w_a2[j], gla_b_a[j], gla_o_g[j],
                                 mem_kv, memq_g[i], memk_g[i], batch, seq, mem_len)
        w_o = w_out[i].astype(BF16)
        x_mid, hn, logits = _out_proj(mix, mo, w_o[:mix.shape[1]], w_o[mix.shape[1]:], x2, ffn_norm_g[i],
                                      _router_weight(w_grp[i], w_exp[i]))
        x2 = _moe_layer(x_mid, hn, logits, b_grp[i], b_exp[i], w_gate, w_up, w_down, i)
    return x2.reshape(batch, seq, d)
```

```python
import functools
import math

import jax
import jax.numpy as jnp
from jax import lax
from jax.experimental import pallas as pl
from jax.experimental.pallas import tpu as pltpu

F32 = jnp.float32
BF16 = jnp.bfloat16
I32 = jnp.int32

HEAD_DIM = 128
FOX_HEADS = 12
FOX_W = FOX_HEADS * HEAD_DIM
MEM_HEADS = 4
MEM_W = MEM_HEADS * HEAD_DIM
GLA_HEADS = 4
GLA_DV = FOX_W // GLA_HEADS
GLA_DK = GLA_DV // 2
GLA_DKP = 256
GLA_RANK = 16
GLA_TAU = 16.0
N_GROUPS = 4
EXPERTS_PER_GROUP = 8
N_EXPERTS = N_GROUPS * EXPERTS_PER_GROUP
RMS_EPS = 1e-6
LOG2E = 1.4426950408889634
GATE_GROUP = 16

LANES = 128
SUBLANES = 8
VMEM_LIMIT_BYTES = 56 * 1024 * 1024

PROJ_TM = 1024
PROJ_TN = 512
FOX_PROJ_TN = 1024
ATTN_T = 512
ATTN_HEADS_PER_STEP = 12
DECAY_TS = 512
MEM_TQ = 512
OUT_TM = 512
ROUTER_TT = 512
ROW_TM = 512
MOE_BLOCK = 256
ROW_UNROLL = 16
W_CHUNKS = 12
W_PHASE_CHUNKS = 3
W_CHUNK_SPLIT = 1
GLA_TS = 256
GLA_CHUNK = 64
ROUTER_GROUP_ROW = 0
ROUTER_EXPERT_ROW = 8
NEG_BIG = -1e30


def _params(*sem):
    return pltpu.CompilerParams(dimension_semantics=sem, vmem_limit_bytes=VMEM_LIMIT_BYTES)


def _log_sigmoid(z):
    return jnp.minimum(z, 0.0) - jnp.log1p(jnp.exp(-jnp.abs(z)))


def _dot(a, b):
    return jnp.dot(a, b, preferred_element_type=F32)


def _dot_nt(a, b):
    return lax.dot_general(a, b, (((1,), (1,)), ((), ())), preferred_element_type=F32)


def _dot_f32(a, b):
    return jnp.dot(a, b, preferred_element_type=F32, precision=lax.Precision.HIGHEST)


def _pack_halves(x):
    n = x.shape[1] // 2
    lo = lax.bitcast_convert_type(x[:, :n].astype(BF16).astype(F32), I32)
    hi = lax.bitcast_convert_type(x[:, n:].astype(BF16).astype(F32), I32)
    return lax.shift_right_logical(lo, 16) | hi


def _unpack_halves(w):
    lo = lax.bitcast_convert_type(lax.shift_left(w, 16), F32)
    hi = lax.bitcast_convert_type(w & jnp.int32(-65536), F32)
    return jnp.concatenate([lo.astype(BF16), hi.astype(BF16)], axis=1)


def _split3(x):
    hi = x.astype(BF16)
    r1 = x - hi.astype(F32)
    mid = r1.astype(BF16)
    lo = (r1 - mid.astype(F32)).astype(BF16)
    return hi, mid, lo


def _norm_matmul_kernel(x_ref, g_ref, w_ref, hg_ref, ws_ref, o_ref, os_ref, xn_ref, *, nj_main, n_norm_tiles):
    j = pl.program_id(1)

    @pl.when(j == 0)
    def _():
        x = x_ref[...]
        ms = jnp.mean(x * x, axis=-1, keepdims=True)
        xn_ref[...] = (x * lax.rsqrt(ms + RMS_EPS) * g_ref[...]).astype(BF16)

    if n_norm_tiles > 0:
        @pl.when(j < n_norm_tiles)
        def _():
            acc = _dot(xn_ref[...], w_ref[...])
            for h in range(acc.shape[1] // HEAD_DIM):
                sl = slice(h * HEAD_DIM, (h + 1) * HEAD_DIM)
                a = acc[:, sl]
                ms = jnp.mean(a * a, axis=-1, keepdims=True)
                o_ref[:, sl] = (a * lax.rsqrt(ms + RMS_EPS) * hg_ref[:, sl]).astype(o_ref.dtype)

    @pl.when((j >= n_norm_tiles) & (j < nj_main))
    def _():
        o_ref[...] = _dot(xn_ref[...], w_ref[...]).astype(o_ref.dtype)

    @pl.when(j == nj_main)
    def _():
        os_ref[...] = _dot(xn_ref[...], ws_ref[...])


def _norm_matmul(x, g, w_main, head_gain, w_side, *, n_norm_tiles, tm, name, tn=PROJ_TN):
    t, d = x.shape
    n_main = w_main.shape[1]
    nj_main = n_main // tn
    n_side = w_side.shape[1]
    assert t % tm == 0 and n_main % tn == 0 and n_side % LANES == 0
    last = nj_main - 1
    kernel = functools.partial(_norm_matmul_kernel, nj_main=nj_main, n_norm_tiles=n_norm_tiles)
    return pl.pallas_call(
        kernel,
        grid=(t // tm, nj_main + 1),
        in_specs=[
            pl.BlockSpec((tm, d), lambda i, j: (i, 0)),
            pl.BlockSpec((1, d), lambda i, j: (0, 0)),
            pl.BlockSpec((d, tn), lambda i, j: (0, jnp.minimum(j, last))),
            pl.BlockSpec((1, tn), lambda i, j: (0, jnp.minimum(j, last))),
            pl.BlockSpec((d, n_side), lambda i, j: (0, 0)),
        ],
        out_specs=[
            pl.BlockSpec((tm, tn), lambda i, j: (i, jnp.minimum(j, last))),
            pl.BlockSpec((tm, n_side), lambda i, j: (i, 0)),
        ],
        out_shape=[jax.ShapeDtypeStruct((t, n_main), BF16), jax.ShapeDtypeStruct((t, n_side), F32)],
        scratch_shapes=[pltpu.VMEM((tm, d), BF16)],
        compiler_params=_params("parallel", "arbitrary"),
        name=name,
    )(x, g.reshape(1, d), w_main, head_gain, w_side)


def _fox_decay_kernel(f_ref, b_ref, kb_ref, carry_ref):
    @pl.when(pl.program_id(1) == 0)
    def _():
        carry_ref[...] = jnp.zeros_like(carry_ref)

    lane = lax.broadcasted_iota(I32, f_ref.shape, 1)
    f = jnp.where(lane < FOX_HEADS, f_ref[...], 0.0)
    f = f + pltpu.roll(f, GATE_GROUP, 1) + pltpu.roll(f, 2 * GATE_GROUP, 1)
    lf = _log_sigmoid(f + b_ref[...]) * (-LOG2E)
    ts = lf.shape[0]
    r = lax.broadcasted_iota(I32, (ts, ts), 0)
    c = lax.broadcasted_iota(I32, (ts, ts), 1)
    incl = jnp.where(c <= r, 1.0, 0.0).astype(BF16)
    hi, mid, lo = _split3(lf)
    cs = _dot(incl, hi) + _dot(incl, mid) + _dot(incl, lo) + carry_ref[0:1, :]
    carry_ref[...] = jnp.broadcast_to(cs[ts - 1:ts, :], carry_ref.shape)
    hi, mid, lo = _split3(cs)
    piece = jnp.where(lane < GATE_GROUP, hi.astype(F32),
                      jnp.where(lane < 2 * GATE_GROUP, mid.astype(F32),
                                jnp.where(lane < 3 * GATE_GROUP, lo.astype(F32), 0.0)))
    kb_ref[...] = piece.astype(BF16)


def _fox_decay(f_side, col_block, b_f, batch, seq):
    ts = DECAY_TS
    ns = seq // ts
    b_pad = jnp.zeros((LANES,), F32)
    for p in range(3):
        b_pad = b_pad.at[p * GATE_GROUP:p * GATE_GROUP + FOX_HEADS].set(b_f)
    return pl.pallas_call(
        _fox_decay_kernel,
        grid=(batch, ns),
        in_specs=[
            pl.BlockSpec((ts, LANES), lambda b, s: (b * ns + s, col_block)),
            pl.BlockSpec((1, LANES), lambda b, s: (0, 0)),
        ],
        out_specs=pl.BlockSpec((ts, LANES), lambda b, s: (b * ns + s, 0)),
        out_shape=jax.ShapeDtypeStruct((batch * seq, LANES), BF16),
        scratch_shapes=[pltpu.VMEM((SUBLANES, LANES), F32)],
        compiler_params=_params("parallel", "arbitrary"),
        name="fox_decay",
    )(f_side, b_pad.reshape(1, LANES))


def _fox_attn_kernel(q_ref, k_ref, v_ref, kb_ref, og_ref, o_ref, m_ref, acc_ref):
    hp = pl.program_id(1)
    qi = pl.program_id(2)
    t = q_ref.shape[0]
    lane = lax.broadcasted_iota(I32, (t, LANES), 1)
    ones = jnp.ones((t, HEAD_DIM), BF16)
    q_aug = []
    for hh in range(ATTN_HEADS_PER_STEP):
        h = hp * ATTN_HEADS_PER_STEP + hh
        own = jnp.where(lane < 3 * GATE_GROUP, jnp.where((lane & (GATE_GROUP - 1)) == h, 1.0, 0.0), 0.0)
        q_aug.append(jnp.concatenate([q_ref[:, hh * HEAD_DIM:(hh + 1) * HEAD_DIM], own.astype(BF16)], axis=1))
    m_ref[...] = jnp.full_like(m_ref, NEG_BIG)
    acc_ref[...] = jnp.zeros_like(acc_ref)

    def tile(j, masked):
        rows = pl.ds(pl.multiple_of(j * t, t), t)
        kb = kb_ref[rows, :]
        for hh in range(ATTN_HEADS_PER_STEP):
            cols = slice(hh * HEAD_DIM, (hh + 1) * HEAD_DIM)
            k_aug = jnp.concatenate([k_ref[rows, cols], kb], axis=1)
            s = _dot_nt(q_aug[hh], k_aug)
            if masked:
                row = lax.broadcasted_iota(I32, (t, t), 0)
                col = lax.broadcasted_iota(I32, (t, t), 1)
                s = jnp.where(col <= row, s, NEG_BIG)
            m_prev = m_ref[hh]
            m_new = jnp.maximum(m_prev, jnp.max(s, axis=-1, keepdims=True))
            alpha = jnp.exp2(m_prev - m_new)
            p = jnp.exp2(s - jnp.concatenate([m_new] * (t // LANES), axis=1))
            v_aug = jnp.concatenate([v_ref[rows, cols], ones], axis=1)
            pv = _dot(p.astype(BF16), v_aug)
            acc_ref[hh] = jnp.concatenate([alpha, alpha], axis=1) * acc_ref[hh] + pv
            m_ref[hh] = m_new

    def full_tile(j, carry):
        tile(j, False)
        return carry

    lax.fori_loop(0, qi, full_tile, 0)
    tile(qi, True)
    for hh in range(ATTN_HEADS_PER_STEP):
        cols = slice(hh * HEAD_DIM, (hh + 1) * HEAD_DIM)
        acc = acc_ref[hh]
        og = og_ref[:, cols].astype(F32)
        o_ref[:, cols] = (acc[:, :HEAD_DIM] / acc[:, HEAD_DIM:] * jax.nn.sigmoid(og)).astype(o_ref.dtype)


def _fox_attention(proj, kb, batch, seq):
    t = ATTN_T
    nq = seq // t
    hps = ATTN_HEADS_PER_STEP
    w = hps * HEAD_DIM
    nhp = FOX_HEADS // hps
    return pl.pallas_call(
        _fox_attn_kernel,
        grid=(batch, nhp, nq),
        in_specs=[
            pl.BlockSpec((t, w), lambda b, h, i: (b * nq + i, h)),
            pl.BlockSpec((seq, w), lambda b, h, i: (b, nhp + h), pipeline_mode=pl.Buffered(1)),
            pl.BlockSpec((seq, w), lambda b, h, i: (b, 2 * nhp + h), pipeline_mode=pl.Buffered(1)),
            pl.BlockSpec((seq, LANES), lambda b, h, i: (b, 0), pipeline_mode=pl.Buffered(1)),
            pl.BlockSpec((t, w), lambda b, h, i: (b * nq + i, 3 * nhp + h)),
        ],
        out_specs=pl.BlockSpec((t, w), lambda b, h, i: (b * nq + i, h)),
        out_shape=jax.ShapeDtypeStruct((batch * seq, FOX_W), BF16),
        scratch_shapes=[pltpu.VMEM((hps, t, LANES), F32), pltpu.VMEM((hps, t, 2 * HEAD_DIM), F32)],
        compiler_params=_params("parallel", "parallel", "arbitrary"),
        name="fox_attention",
    )(proj, proj, proj, kb, proj)


def _mem_attn_kernel(q_ref, k_ref, v_ref, qg_ref, kg_ref, o_ref, *, q_offset):
    for h in range(MEM_HEADS):
        sl = slice(h * HEAD_DIM, (h + 1) * HEAD_DIM)
        q = q_ref[:, q_offset + h * HEAD_DIM:q_offset + (h + 1) * HEAD_DIM].astype(F32)
        q = q * lax.rsqrt(jnp.mean(q * q, axis=-1, keepdims=True) + RMS_EPS) * qg_ref[...]
        k = k_ref[:, sl].astype(F32)
        k = k * lax.rsqrt(jnp.mean(k * k, axis=-1, keepdims=True) + RMS_EPS) * kg_ref[...]
        s = _dot_nt(q.astype(BF16), k.astype(BF16))
        p = jnp.exp(s - jnp.max(s, axis=-1, keepdims=True))
        o = _dot(p.astype(BF16), v_ref[:, sl]) / jnp.sum(p, axis=-1, keepdims=True)
        o_ref[:, sl] = o.astype(o_ref.dtype)


def _memory_attention(q_src, q_block_w, q_col_block, q_offset, mem_kv, q_gain, k_gain, batch, seq, mem_len):
    tq = MEM_TQ
    nq = seq // tq
    scale = 1.0 / math.sqrt(HEAD_DIM)
    return pl.pallas_call(
        functools.partial(_mem_attn_kernel, q_offset=q_offset),
        grid=(batch * nq,),
        in_specs=[
            pl.BlockSpec((tq, q_block_w), lambda i: (i, q_col_block)),
            pl.BlockSpec((mem_len, MEM_W), lambda i: (i // nq, 0)),
            pl.BlockSpec((mem_len, MEM_W), lambda i: (i // nq, 1)),
            pl.BlockSpec((1, HEAD_DIM), lambda i: (0, 0)),
            pl.BlockSpec((1, HEAD_DIM), lambda i: (0, 0)),
        ],
        out_specs=pl.BlockSpec((tq, MEM_W), lambda i: (i, 0)),
        out_shape=jax.ShapeDtypeStruct((batch * seq, MEM_W), BF16),
        compiler_params=_params("parallel"),
        name="memory_attention",
    )(q_src, mem_kv, mem_kv, (q_gain * scale).reshape(1, HEAD_DIM), k_gain.reshape(1, HEAD_DIM))


def _gla_kernel(q_ref, k_ref, v_ref, r_ref, a_ref, wa_ref, ba_ref, og_ref, o_ref,
                st_ref, qe_ref, q2_ref, k2_ref, kd_ref, oi_ref):
    @pl.when(pl.program_id(1) == 0)
    def _():
        st_ref[...] = jnp.zeros_like(st_ref)

    ts = q_ref.shape[0]
    c = GLA_CHUNK
    shift = c.bit_length() - 1
    n_chunks = ts // c

    z = _dot_f32(a_ref[...], wa_ref[...]) + ba_ref[...]
    g = _log_sigmoid(z) * (1.0 / GLA_TAU)
    ri = lax.broadcasted_iota(I32, (ts, ts), 0)
    ci = lax.broadcasted_iota(I32, (ts, ts), 1)
    chunk_start = lax.shift_left(lax.shift_right_logical(ri, shift), shift)

    def chunk_causal(val):
        return jnp.where(ci <= ri, jnp.where(ci >= chunk_start, val, 0.0), 0.0)

    incl = chunk_causal(jnp.ones((ts, ts), F32)).astype(BF16)
    hi, mid, lo = _split3(g)
    bc = _dot(incl, hi) + _dot(incl, mid) + _dot(incl, lo)
    decay = []
    for ic in range(n_chunks):
        rows = slice(ic * c, (ic + 1) * c)
        bcc = bc[rows]
        b_mid = bcc[c // 2 - 1:c // 2]
        b_last = bcc[c - 1:c]
        q = q_ref[rows, :].astype(F32)
        k = k_ref[rows, :].astype(F32)
        qe_ref[rows, :] = (q * jnp.exp(bcc)).astype(BF16)
        q2_ref[rows, :] = (q * jnp.exp(bcc - b_mid)).astype(BF16)
        k2_ref[rows, :] = (k * jnp.exp(b_mid - bcc)).astype(BF16)
        kd_ref[rows, :] = (k * jnp.exp(b_last - bcc)).astype(BF16)
        decay.append(jnp.exp(b_last))
    for h in range(GLA_HEADS):
        ks = slice(h * GLA_DKP, (h + 1) * GLA_DKP)
        vs = slice(h * GLA_DV, (h + 1) * GLA_DV)
        att = chunk_causal(_dot_nt(q2_ref[:, ks], k2_ref[:, ks]))
        oi_ref[:, vs] = _dot(att.astype(BF16), v_ref[:, vs])

    for ic in range(n_chunks):
        rows = slice(ic * c, (ic + 1) * c)
        for h in range(GLA_HEADS):
            ks = slice(h * GLA_DKP, (h + 1) * GLA_DKP)
            vs = slice(h * GLA_DV, (h + 1) * GLA_DV)
            st = st_ref[h]
            o = _dot_nt(qe_ref[rows, ks], st.astype(BF16)) + oi_ref[rows, vs]
            upd = lax.dot_general(v_ref[rows, vs], kd_ref[rows, ks], (((0,), (0,)), ((), ())),
                                  preferred_element_type=F32)
            st_ref[h] = st * decay[ic][:, ks] + upd
            on = o * lax.rsqrt(jnp.mean(o * o, axis=-1, keepdims=True) + RMS_EPS) * og_ref[...]
            rh = r_ref[rows, vs].astype(F32)
            o_ref[rows, vs] = (on * rh * jax.nn.sigmoid(rh)).astype(o_ref.dtype)


def _gla(proj, a_side, wa, ba, o_gain, batch, seq):
    ts = GLA_TS
    ns = seq // ts
    kw = GLA_HEADS * GLA_DKP
    vw = GLA_HEADS * GLA_DV
    assert (2 * vw) % kw == 0
    q_blk = (2 * vw) // kw
    return pl.pallas_call(
        _gla_kernel,
        grid=(batch, ns),
        in_specs=[
            pl.BlockSpec((ts, kw), lambda b, s: (b * ns + s, q_blk)),
            pl.BlockSpec((ts, kw), lambda b, s: (b * ns + s, q_blk + 1)),
            pl.BlockSpec((ts, vw), lambda b, s: (b * ns + s, 0)),
            pl.BlockSpec((ts, vw), lambda b, s: (b * ns + s, 1)),
            pl.BlockSpec((ts, LANES), lambda b, s: (b * ns + s, 0)),
            pl.BlockSpec((LANES, kw), lambda b, s: (0, 0)),
            pl.BlockSpec((1, kw), lambda b, s: (0, 0)),
            pl.BlockSpec((1, GLA_DV), lambda b, s: (0, 0)),
        ],
        out_specs=pl.BlockSpec((ts, vw), lambda b, s: (b * ns + s, 0)),
        out_shape=jax.ShapeDtypeStruct((batch * seq, vw), BF16),
        scratch_shapes=[pltpu.VMEM((GLA_HEADS, GLA_DV, GLA_DKP), F32)] + [pltpu.VMEM((ts, kw), BF16)] * 4
        + [pltpu.VMEM((ts, vw), F32)],
        compiler_params=_params("parallel", "arbitrary"),
        name="gla",
    )(proj, proj, proj, proj, a_side, wa, ba, o_gain.reshape(1, GLA_DV))


def _out_proj_kernel(mix_ref, mo_ref, w1_ref, w2_ref, x_ref, g_ref, wr_ref, xo_ref, hn_ref, lg_ref):
    y = _dot(mix_ref[...], w1_ref[...]) + _dot(mo_ref[...], w2_ref[...])
    xn = x_ref[...] + y
    xo_ref[...] = xn
    hn = xn * lax.rsqrt(jnp.mean(xn * xn, axis=-1, keepdims=True) + RMS_EPS) * g_ref[...]
    hn_ref[...] = _pack_halves(hn)
    h_hi = hn.astype(BF16)
    h_lo = (hn - h_hi.astype(F32)).astype(BF16)
    both = _dot(h_hi, wr_ref[...])
    lg_ref[...] = both[:, :LANES] + both[:, LANES:] + _dot(h_lo, wr_ref[:, :LANES])


def _out_proj(mix, mo, w1, w2, x, g, wr):
    t, d = x.shape
    tm = OUT_TM
    once = pl.Buffered(1)
    return pl.pallas_call(
        _out_proj_kernel,
        grid=(t // tm,),
        in_specs=[
            pl.BlockSpec((tm, mix.shape[1]), lambda i: (i, 0)),
            pl.BlockSpec((tm, mo.shape[1]), lambda i: (i, 0)),
            pl.BlockSpec(w1.shape, lambda i: (0, 0), pipeline_mode=once),
            pl.BlockSpec(w2.shape, lambda i: (0, 0), pipeline_mode=once),
            pl.BlockSpec((tm, d), lambda i: (i, 0)),
            pl.BlockSpec((1, d), lambda i: (0, 0), pipeline_mode=once),
            pl.BlockSpec((d, 2 * LANES), lambda i: (0, 0), pipeline_mode=once),
        ],
        out_specs=[
            pl.BlockSpec((tm, d), lambda i: (i, 0)),
            pl.BlockSpec((tm, d // 2), lambda i: (i, 0)),
            pl.BlockSpec((tm, LANES), lambda i: (i, 0)),
        ],
        out_shape=[jax.ShapeDtypeStruct((t, d), F32), jax.ShapeDtypeStruct((t, d // 2), I32),
                   jax.ShapeDtypeStruct((t, LANES), F32)],
        compiler_params=_params("parallel"),
        name="out_proj",
    )(mix, mo, w1, w2, x, g.reshape(1, d), wr)


def _router_kernel(lg_ref, b_ref, o_ref, cnt_ref, carry_ref):
    @pl.when(pl.program_id(0) == 0)
    def _():
        carry_ref[...] = jnp.zeros_like(carry_ref)

    logits = lg_ref[...].T + b_ref[:, 0:1]
    tt = logits.shape[1]
    gl = [logits[ROUTER_GROUP_ROW + g:ROUTER_GROUP_ROW + g + 1, :] for g in range(N_GROUPS)]
    best = gl[0]
    gidx = jnp.zeros((1, tt), I32)
    for g in range(1, N_GROUPS):
        better = gl[g] > best
        best = jnp.where(better, gl[g], best)
        gidx = jnp.where(better, g, gidx)
    denom = jnp.exp(gl[0] - best)
    for g in range(1, N_GROUPS):
        denom = denom + jnp.exp(gl[g] - best)
    g_p = 1.0 / denom

    epg = EXPERTS_PER_GROUP
    in_grp = logits[ROUTER_EXPERT_ROW:ROUTER_EXPERT_ROW + epg, :]
    for g in range(1, N_GROUPS):
        lo = ROUTER_EXPERT_ROW + g * epg
        in_grp = jnp.where(gidx == g, logits[lo:lo + epg, :], in_grp)
    rows = lax.broadcasted_iota(I32, (epg, tt), 0).astype(F32)
    m1 = jnp.max(in_grp, axis=0, keepdims=True)
    i1 = jnp.min(jnp.where(in_grp == m1, rows, float(epg)), axis=0, keepdims=True)
    rest = jnp.where(rows == i1, -jnp.inf, in_grp)
    m2 = jnp.max(rest, axis=0, keepdims=True)
    i2 = jnp.min(jnp.where(rest == m2, rows, float(epg)), axis=0, keepdims=True)
    e21 = jnp.exp(m2 - m1)
    w1 = g_p / (1.0 + e21)
    w2 = g_p * e21 / (1.0 + e21)
    e1 = gidx * epg + i1.astype(I32)
    e2 = gidx * epg + i2.astype(I32)

    er = lax.broadcasted_iota(I32, (N_EXPERTS, tt), 0)
    oh1 = er == e1
    oh2 = er == e2
    cnt = jnp.where(oh1, 1.0, 0.0) + jnp.where(oh2, 1.0, 0.0)
    r = lax.broadcasted_iota(I32, (tt, tt), 0)
    c = lax.broadcasted_iota(I32, (tt, tt), 1)
    strict = jnp.where(r < c, 1.0, 0.0).astype(BF16)
    before = _dot(cnt.astype(BF16), strict) + carry_ref[:, 0:1]
    rank1 = jnp.sum(jnp.where(oh1, before, 0.0), axis=0, keepdims=True)
    rank2 = jnp.sum(jnp.where(oh2, before, 0.0), axis=0, keepdims=True)
    total = carry_ref[...] + jnp.sum(cnt, axis=1, keepdims=True)
    carry_ref[...] = total
    cnt_ref[...] = total

    o_ref[0:1, :] = e1.astype(F32)
    o_ref[1:2, :] = e2.astype(F32)
    o_ref[2:3, :] = rank1
    o_ref[3:4, :] = rank2
    o_ref[4:5, :] = w1
    o_ref[5:6, :] = w2
    o_ref[6:8, :] = jnp.zeros((2, tt), F32)


def _router(logits, bias_col):
    t = logits.shape[0]
    tt = ROUTER_TT
    return pl.pallas_call(
        _router_kernel,
        grid=(t // tt,),
        in_specs=[
            pl.BlockSpec((tt, LANES), lambda i: (i, 0)),
            pl.BlockSpec((LANES, LANES), lambda i: (0, 0)),
        ],
        out_specs=[
            pl.BlockSpec((SUBLANES, tt), lambda i: (0, i)),
            pl.BlockSpec((N_EXPERTS, LANES), lambda i: (0, 0)),
        ],
        out_shape=[jax.ShapeDtypeStruct((SUBLANES, t), F32), jax.ShapeDtypeStruct((N_EXPERTS, LANES), F32)],
        scratch_shapes=[pltpu.VMEM((N_EXPERTS, LANES), F32)],
        compiler_params=_params("arbitrary"),
        name="router",
    )(logits, bias_col)


TAB_EXPERT, TAB_VALID, TAB_FIRST, TAB_ORD, TAB_SLOT, TAB_NEXT, TAB_DONE, TAB_ROWS = range(8)


def _plan_kernel(r_ref, tot_ref, dest_ref, tab_ref):
    ne = N_EXPERTS
    blk = float(MOE_BLOCK)
    counts = tot_ref[...]
    nblk = jnp.floor((counts + (blk - 1.0)) * (1.0 / blk))
    padded = nblk * blk
    er = lax.broadcasted_iota(I32, (ne, ne), 0)
    ec = lax.broadcasted_iota(I32, (ne, ne), 1)
    tri = jnp.where(ec <= er, 1.0, 0.0)
    pad_ends = _dot_f32(tri, padded)
    pad_starts = pad_ends - padded
    ps_col = pad_starts[:, 0:1]

    tt = r_ref.shape[1]
    eid = lax.broadcasted_iota(I32, (ne, tt), 0).astype(F32)
    for k in range(2):
        start = jnp.sum(jnp.where(eid == r_ref[k:k + 1, :], ps_col, 0.0), axis=0, keepdims=True)
        dest_ref[k:k + 1, :] = (start + r_ref[k + 2:k + 3, :]).astype(I32)

    @pl.when(pl.program_id(0) == 0)
    def _():
        n_phases = float(W_CHUNKS // W_PHASE_CHUNKS)
        nonempty = jnp.where(nblk > 0.0, 1.0, 0.0)
        rank = _dot_f32(tri, nonempty) - 1.0
        slot_e = rank - 2.0 * jnp.floor(rank * 0.5)
        lane0 = jnp.where(lax.broadcasted_iota(I32, (SUBLANES, LANES), 1) == 0, 1.0, 0.0)
        ne_row = _dot_nt(lane0, nonempty)[0:1, :]
        nb_row = _dot_nt(lane0, nblk)[0:1, :]
        ecf = ec.astype(F32)
        cand = jnp.where(ec > er, jnp.where(ne_row > 0.0, ecf, float(ne)), float(ne))
        nxt_e = jnp.min(cand, axis=1, keepdims=True)
        nxt_e = jnp.where(nxt_e == float(ne), -1.0, nxt_e)
        cand = jnp.where(ec < er, jnp.where(ne_row > 0.0, ecf, -1.0), -1.0)
        prev_e = jnp.max(cand, axis=1, keepdims=True)
        nb_prev = jnp.sum(jnp.where(ecf == prev_e, nb_row, 0.0), axis=1, keepdims=True)
        done_e = jnp.where(prev_e >= 0.0, jnp.minimum(nb_prev, n_phases), 0.0)

        nbl = tab_ref.shape[1]
        brow = lax.broadcasted_iota(I32, (1, nbl), 1).astype(F32) * blk
        be = jnp.sum(jnp.where(brow >= pad_ends[:, 0:1], 1.0, 0.0), axis=0, keepdims=True)
        be = jnp.minimum(be, float(ne - 1))
        valid = jnp.where(brow < pad_ends[ne - 1:ne, 0:1], 1.0, 0.0)
        own = lax.broadcasted_iota(I32, (ne, nbl), 0).astype(F32) == be

        def pick(col):
            return jnp.sum(jnp.where(own, col, 0.0), axis=0, keepdims=True)

        ordinal = (brow - pick(ps_col)) * (1.0 / blk)
        first = valid * jnp.where(ordinal == 0.0, 1.0, 0.0)
        used = jnp.clip(pick(counts[:, 0:1]) - ordinal * blk, 0.0, blk)
        rows = {TAB_EXPERT: be, TAB_VALID: valid, TAB_FIRST: first, TAB_ORD: ordinal, TAB_SLOT: pick(slot_e[:, 0:1]),
                TAB_NEXT: pick(nxt_e), TAB_DONE: pick(done_e), TAB_ROWS: used}
        assert len(rows) == tab_ref.shape[0]
        for i, row in rows.items():
            tab_ref[i:i + 1, :] = row.astype(I32)


def _plan(routed, totals, n_blocks):
    t = routed.shape[1]
    tt = ROUTER_TT
    nbl = -(-n_blocks // LANES) * LANES
    return pl.pallas_call(
        _plan_kernel,
        grid=(t // tt,),
        in_specs=[
            pl.BlockSpec((SUBLANES, tt), lambda i: (0, i)),
            pl.BlockSpec((N_EXPERTS, LANES), lambda i: (0, 0)),
        ],
        out_specs=[
            pl.BlockSpec((2, tt), lambda i: (0, i)),
            pl.BlockSpec((SUBLANES, nbl), lambda i: (0, 0)),
        ],
        out_shape=[jax.ShapeDtypeStruct((2, t), I32), jax.ShapeDtypeStruct((SUBLANES, nbl), I32)],
        compiler_params=_params("arbitrary"),
        name="moe_plan",
    )(routed, totals)


def _row_copy(src_ref, src_row, dst_ref, dst_row, sem):
    return pltpu.make_async_copy(src_ref.at[pl.ds(src_row, 1), :], dst_ref.at[pl.ds(dst_row, 1), :], sem)


def _dispatch_kernel(d1_ref, d2_ref, tab_ref, h_ref, xs_ref, zero_ref, sem, zsem):
    tm = h_ref.shape[0]
    base = pl.program_id(0) * tm

    @pl.when(pl.program_id(0) == 0)
    def _():
        zero_ref[...] = jnp.zeros_like(zero_ref)

        def per_block(act):
            def body(b, carry):
                block_rows = xs_ref.at[pl.ds(pl.multiple_of(b * MOE_BLOCK, MOE_BLOCK), MOE_BLOCK), :]

                @pl.when(tab_ref[TAB_VALID, b] == 0)
                def _():
                    act(pltpu.make_async_copy(zero_ref, block_rows, zsem))

                @pl.when(tab_ref[TAB_VALID, b] != 0)
                def _():
                    used = tab_ref[TAB_ROWS, b]
                    head = (-used) & (SUBLANES - 1)
                    for r in range(SUBLANES - 1):
                        @pl.when(r < head)
                        def _():
                            act(_row_copy(zero_ref, 0, xs_ref, b * MOE_BLOCK + used + r, zsem))

                    rest = MOE_BLOCK - used - head
                    pos = b * MOE_BLOCK + used + head
                    for bit in range(SUBLANES.bit_length() - 1, MOE_BLOCK.bit_length() - 1):
                        size = 1 << bit

                        @pl.when((rest & size) != 0)
                        def _():
                            act(pltpu.make_async_copy(zero_ref.at[pl.ds(0, size), :],
                                                      xs_ref.at[pl.ds(pl.multiple_of(pos, SUBLANES), size), :], zsem))

                        pos = pos + (rest & size)

                return carry

            lax.fori_loop(0, xs_ref.shape[0] // MOE_BLOCK, body, 0)

        per_block(lambda cp: cp.start())
        per_block(lambda cp: cp.wait())

    def issue(g, carry):
        for k in range(ROW_UNROLL):
            r = g * ROW_UNROLL + k
            _row_copy(h_ref, r, xs_ref, d1_ref[base + r], sem).start()
            _row_copy(h_ref, r, xs_ref, d2_ref[base + r], sem).start()
        return carry

    lax.fori_loop(0, tm // ROW_UNROLL, issue, 0)

    def drain(g, carry):
        for k in range(ROW_UNROLL):
            r = g * ROW_UNROLL + k
            _row_copy(h_ref, r, xs_ref, 0, sem).wait()
            _row_copy(h_ref, r, xs_ref, 0, sem).wait()
        return carry

    lax.fori_loop(0, tm // ROW_UNROLL, drain, 0)


def _dispatch(dest1, dest2, tables, hn, p_rows):
    t, d = hn.shape
    tm = ROW_TM
    grid_spec = pltpu.PrefetchScalarGridSpec(
        num_scalar_prefetch=3,
        grid=(t // tm,),
        in_specs=[pl.BlockSpec((tm, d), lambda i, *_: (i, 0))],
        out_specs=pl.BlockSpec(memory_space=pl.ANY),
        scratch_shapes=[pltpu.VMEM((MOE_BLOCK, d), hn.dtype), pltpu.SemaphoreType.DMA, pltpu.SemaphoreType.DMA],
    )
    return pl.pallas_call(
        _dispatch_kernel,
        grid_spec=grid_spec,
        out_shape=jax.ShapeDtypeStruct((p_rows, d), hn.dtype),
        compiler_params=_params("arbitrary"),
        name="moe_dispatch",
    )(dest1, dest2, tables, hn)


def _weight_chunk(c, d, de):
    per = W_CHUNKS // 3
    if c < 2 * per:
        return c // per, (c % per) * (d // per), 0
    r, h = divmod(c - 2 * per, d // de)
    return 2, r * (d // per), h * de


def _moe_kernel(tab_ref, x_ref, wg_hbm, wu_hbm, wd_hbm, y_ref, wg_c, wu_c, wd_c, stage_ref, sems, *, layer):
    b = pl.program_id(0)
    expert = tab_ref[TAB_EXPERT, b]
    valid = tab_ref[TAB_VALID, b]
    first = tab_ref[TAB_FIRST, b]
    ordinal = tab_ref[TAB_ORD, b]
    slot = tab_ref[TAB_SLOT, b]
    nxt = tab_ref[TAB_NEXT, b]
    done = tab_ref[TAB_DONE, b]
    hbm = (wg_hbm, wu_hbm, wd_hbm)
    cache = (wg_c, wu_c, wd_c)
    d, de = wg_c.shape[1], wg_c.shape[2]
    rows, cols = stage_ref.shape[1], stage_ref.shape[2]

    sub = rows // W_CHUNK_SPLIT

    def chunk_copies(e, c, k):
        ti, r0, c0 = _weight_chunk(c, d, de)
        return [pltpu.make_async_copy(hbm[ti].at[layer, e, pl.ds(r0 + i * sub, sub), pl.ds(c0, cols)],
                                      stage_ref.at[k, pl.ds(i * sub, sub), :], sems.at[k])
                for i in range(W_CHUNK_SPLIT)]

    def start_phase(e, p):
        for k in range(W_PHASE_CHUNKS):
            for cp in chunk_copies(e, p * W_PHASE_CHUNKS + k, k):
                cp.start()

    def finish_phase(e, s, p):
        for k in range(W_PHASE_CHUNKS):
            c = p * W_PHASE_CHUNKS + k
            for cp in chunk_copies(e, c, k):
                cp.wait()
            ti, r0, c0 = _weight_chunk(c, d, de)
            cache[ti][s, pl.ds(r0, rows), pl.ds(c0, cols)] = stage_ref[k].astype(BF16)

    n_phases = W_CHUNKS // W_PHASE_CHUNKS

    @pl.when(valid != 0)
    def _():
        prev = jnp.maximum(b - 1, 0)
        prev_ord = tab_ref[TAB_ORD, prev]
        prev_nxt = tab_ref[TAB_NEXT, prev]
        pending = (b > 0) & (prev_ord < n_phases) & (prev_nxt >= 0)
        for p in range(n_phases):
            @pl.when(pending & (prev_ord == p))
            def _():
                finish_phase(prev_nxt, 1 - tab_ref[TAB_SLOT, prev], p)

        @pl.when(first != 0)
        def _():
            for p in range(n_phases):
                @pl.when(done <= p)
                def _():
                    start_phase(expert, p)
                    finish_phase(expert, slot, p)

        prefetch = (ordinal < n_phases) & (nxt >= 0)
        for p in range(n_phases):
            @pl.when(prefetch & (ordinal == p))
            def _():
                start_phase(nxt, p)

        x = _unpack_halves(x_ref[...])
        g = _dot(x, wg_c[slot])
        u = _dot(x, wu_c[slot])
        hid = (g * jax.nn.sigmoid(g) * u).astype(BF16)
        y_ref[...] = _pack_halves(_dot(hid, wd_c[slot]))

    @pl.when(valid == 0)
    def _():
        y_ref[...] = jnp.zeros_like(y_ref)


def _moe(tables, xs, wg, wu, wd, layer):
    p_rows, dh = xs.shape
    tm = MOE_BLOCK
    d, de = wg.shape[2], wg.shape[3]
    assert d == 2 * dh and wd.shape[2:] == (de, d) and d % (W_CHUNKS // 3) == 0 and d % de == 0
    chunk = (d // (W_CHUNKS // 3), de)
    grid_spec = pltpu.PrefetchScalarGridSpec(
        num_scalar_prefetch=1,
        grid=(p_rows // tm,),
        in_specs=[
            pl.BlockSpec((tm, dh), lambda i, *_: (i, 0)),
            pl.BlockSpec(memory_space=pl.ANY),
            pl.BlockSpec(memory_space=pl.ANY),
            pl.BlockSpec(memory_space=pl.ANY),
        ],
        out_specs=pl.BlockSpec((tm, dh), lambda i, *_: (i, 0)),
        scratch_shapes=[
            pltpu.VMEM((2, d, de), BF16), pltpu.VMEM((2, d, de), BF16), pltpu.VMEM((2, de, d), BF16),
            pltpu.VMEM((W_PHASE_CHUNKS,) + chunk, F32), pltpu.SemaphoreType.DMA((W_PHASE_CHUNKS,)),
        ],
    )
    return pl.pallas_call(
        functools.partial(_moe_kernel, layer=layer),
        grid_spec=grid_spec,
        out_shape=jax.ShapeDtypeStruct((p_rows, dh), I32),
        compiler_params=_params("arbitrary"),
        name="moe_experts",
    )(tables, xs, wg, wu, wd)


def _combine_kernel(d1_ref, d2_ref, x_ref, w_ref, ys_ref, o_ref, buf, sems):
    tm = x_ref.shape[0]
    i = pl.program_id(0)
    slot = i % 2

    def gather(step, s, start):
        base = step * tm

        def body(g, carry):
            for k in range(ROW_UNROLL):
                r = g * ROW_UNROLL + k
                if start:
                    _row_copy(ys_ref, d1_ref[base + r], buf.at[s, 0], r, sems.at[s]).start()
                    _row_copy(ys_ref, d2_ref[base + r], buf.at[s, 1], r, sems.at[s]).start()
                else:
                    _row_copy(ys_ref, 0, buf.at[s, 0], r, sems.at[s]).wait()
                    _row_copy(ys_ref, 0, buf.at[s, 1], r, sems.at[s]).wait()
            return carry

        lax.fori_loop(0, tm // ROW_UNROLL, body, 0)

    @pl.when(i == 0)
    def _():
        gather(i, slot, True)

    @pl.when(i + 1 < pl.num_programs(0))
    def _():
        gather(i + 1, 1 - slot, True)

    gather(i, slot, False)
    w = w_ref[...]
    y1 = _unpack_halves(buf[slot, 0]).astype(F32)
    y2 = _unpack_halves(buf[slot, 1]).astype(F32)
    o_ref[...] = x_ref[...] + w[:, 0:1] * y1 + w[:, 1:2] * y2


def _combine(dest1, dest2, x, w12, ys):
    t, d = x.shape
    tm = ROW_TM
    dh = ys.shape[1]
    grid_spec = pltpu.PrefetchScalarGridSpec(
        num_scalar_prefetch=2,
        grid=(t // tm,),
        in_specs=[
            pl.BlockSpec((tm, d), lambda i, a, b: (i, 0)),
            pl.BlockSpec((tm, 2), lambda i, a, b: (i, 0)),
            pl.BlockSpec(memory_space=pl.ANY),
        ],
        out_specs=pl.BlockSpec((tm, d), lambda i, a, b: (i, 0)),
        scratch_shapes=[pltpu.VMEM((2, 2, tm, dh), I32), pltpu.SemaphoreType.DMA((2,))],
    )
    return pl.pallas_call(
        _combine_kernel,
        grid_spec=grid_spec,
        out_shape=jax.ShapeDtypeStruct((t, d), F32),
        compiler_params=_params("arbitrary"),
        name="moe_combine",
    )(dest1, dest2, x, w12, ys)


def _pad_cols(w, width):
    return jnp.pad(w, ((0, 0), (0, width - w.shape[1])))


def _moe_layer(x_mid, hn, logits, b_grp, b_exp, w_gate, w_up, w_down, layer):
    t, d = x_mid.shape
    bias = jnp.zeros((LANES,), F32)
    bias = bias.at[ROUTER_GROUP_ROW:ROUTER_GROUP_ROW + N_GROUPS].set(b_grp)
    bias = bias.at[ROUTER_EXPERT_ROW:ROUTER_EXPERT_ROW + N_EXPERTS].set(b_exp)
    routed, totals = _router(logits, jnp.broadcast_to(bias[:, None], (LANES, LANES)))
    p_rows = 2 * t + N_EXPERTS * MOE_BLOCK
    dest, tables = _plan(routed, totals, p_rows // MOE_BLOCK)
    dest1, dest2 = dest[0], dest[1]
    xs = _dispatch(dest1, dest2, tables, hn, p_rows)
    ys = _moe(tables, xs, w_gate, w_up, w_down, layer)
    w12 = jnp.stack([routed[4], routed[5]], axis=-1)
    return _combine(dest1, dest2, x_mid, w12, ys)


def _router_weight(w_grp, w_exp):
    d = w_grp.shape[0]
    wr = jnp.zeros((d, LANES), F32)
    wr = wr.at[:, ROUTER_GROUP_ROW:ROUTER_GROUP_ROW + N_GROUPS].set(w_grp)
    wr = wr.at[:, ROUTER_EXPERT_ROW:ROUTER_EXPERT_ROW + N_EXPERTS].set(w_exp)
    hi = wr.astype(BF16)
    lo = (wr - hi.astype(F32)).astype(BF16)
    return jnp.concatenate([hi, lo], axis=1)


def _fox_layer(x2, attn_g, w_stack, j, b_f, q_g, k_g, mem_kv, memq_g, memk_g, batch, seq, mem_len):
    qkvo = 4 * FOX_W
    w_in = w_stack[j]
    w_main = w_in[:, :qkvo].astype(BF16)
    w_side = jnp.concatenate([w_in[:, qkvo + FOX_HEADS:], w_in[:, qkvo:qkvo + FOX_HEADS]], axis=1)
    w_side = _pad_cols(w_side, MEM_W + LANES).astype(BF16)
    scale = LOG2E / math.sqrt(HEAD_DIM)
    head_gain = jnp.concatenate([jnp.tile(q_g * scale, FOX_HEADS), jnp.tile(k_g, FOX_HEADS),
                                 jnp.ones((qkvo - 2 * FOX_W,), F32)]).reshape(1, -1)
    proj, side = _norm_matmul(x2, attn_g, w_main, head_gain, w_side, n_norm_tiles=2 * FOX_W // FOX_PROJ_TN,
                              tm=PROJ_TM, name="fox_in_proj", tn=FOX_PROJ_TN)
    kb = _fox_decay(side, MEM_W // LANES, b_f, batch, seq)
    mix = _fox_attention(proj, kb, batch, seq)
    mo = _memory_attention(side, MEM_W, 0, 0, mem_kv, memq_g, memk_g, batch, seq, mem_len)
    return mix, mo


def _gla_layer(x2, attn_g, w_in, w_a2, b_a, o_g, mem_kv, memq_g, memk_g, batch, seq, mem_len):
    d = w_in.shape[0]
    kw = GLA_HEADS * GLA_DK
    vw = GLA_HEADS * GLA_DV
    pad = GLA_DKP - GLA_DK

    def pad_heads(w):
        lead = w.shape[0]
        return jnp.pad(w.reshape(lead, GLA_HEADS, GLA_DK), ((0, 0), (0, 0), (0, pad))).reshape(lead, -1)

    wq = pad_heads(w_in[:, :kw] * (1.0 / math.sqrt(GLA_DK)))
    wk = pad_heads(w_in[:, kw:2 * kw])
    rest = w_in[:, 2 * kw:2 * kw + 2 * vw]
    w_a1 = w_in[:, 2 * kw + 2 * vw:2 * kw + 2 * vw + GLA_RANK]
    w_qm = w_in[:, 2 * kw + 2 * vw + GLA_RANK:]
    w_main = jnp.concatenate([rest, wq, wk, w_qm], axis=1).astype(BF16)
    w_side = _pad_cols(w_a1, LANES).astype(BF16)
    head_gain = jnp.ones((1, w_main.shape[1]), F32)
    proj, a_side = _norm_matmul(x2, attn_g, w_main, head_gain, w_side, n_norm_tiles=0, tm=PROJ_TM,
                                name="gla_in_proj")
    wa = jnp.pad(pad_heads(w_a2), ((0, LANES - GLA_RANK), (0, 0)))
    ba = pad_heads(b_a.reshape(1, kw))
    mix = _gla(proj, a_side, wa, ba, o_g, batch, seq)
    q_col_block = (2 * GLA_HEADS * GLA_DKP + 2 * vw) // MEM_W
    mo = _memory_attention(proj, MEM_W, q_col_block, 0, mem_kv, memq_g, memk_g, batch, seq, mem_len)
    return mix, mo


def kernel(x, mem, mem_norm_g, w_mem_kv, attn_norm_g, fox_w_in, fox_b_f, fox_q_g, fox_k_g, gla_w_in, gla_w_a2,
           gla_b_a, gla_o_g, memq_g, memk_g, w_out, ffn_norm_g, w_grp, b_grp, w_exp, b_exp, w_gate, w_up, w_down):
    batch, seq, d = x.shape
    mem_len = mem.shape[1]
    depth = attn_norm_g.shape[0]
    x2 = x.reshape(batch * seq, d)

    n_kv = w_mem_kv.shape[1]
    mem_kv, _ = _norm_matmul(mem.reshape(batch * mem_len, d), mem_norm_g, w_mem_kv.astype(BF16),
                             jnp.ones((1, n_kv), F32), jnp.zeros((d, LANES), BF16), n_norm_tiles=0,
                             tm=batch * mem_len, name="mem_kv_proj")

    for i in range(depth):
        j = i // 2
        if i % 2 == 0:
            mix, mo = _fox_layer(x2, attn_norm_g[i], fox_w_in, j, fox_b_f[j], fox_q_g[j], fox_k_g[j],
                                 mem_kv, memq_g[i], memk_g[i], batch, seq, mem_len)
        else:
            mix, mo = _gla_layer(x2, attn_norm_g[i], gla_w_in[j], gla_w_a2[j], gla_b_a[j], gla_o_g[j],
                                 mem_kv, memq_g[i], memk_g[i], batch, seq, mem_len)
        w_o = w_out[i].astype(BF16)
        x_mid, hn, logits = _out_proj(mix, mo, w_o[:mix.shape[1]], w_o[mix.shape[1]:], x2, ffn_norm_g[i],
                                      _router_weight(w_grp[i], w_exp[i]))
        x2 = _moe_layer(x_mid, hn, logits, b_grp[i], b_exp[i], w_gate, w_up, w_down, i)
    return x2.reshape(batch, seq, d)
```
